```python
import jax, jax.numpy as jnp
from jax import lax
import numpy as np

D_MODEL = 1024
BATCH = 4
SEQ = 4096
DEPTH = 1

CHUNK = 64
N_MEM = 256
DN_HEADS = 4
DN_HEAD_DIM = 128
DN_WIDTH = DN_HEADS * DN_HEAD_DIM
CONV_K = 4
POOL_GROUPS = 4
POOL_WINDOWS = (2, 4, 8, 16)
POOL_WIDTH = D_MODEL // 2
POOL_GROUP_DIM = POOL_WIDTH // POOL_GROUPS
D_FF = 2816
XA_HEADS = 4
XA_HEAD_DIM = D_MODEL // XA_HEADS
LN_EPS = 1e-5
RMS_EPS = 1e-6
L2_EPS = 1e-6
ALPHA = (2.0 * DEPTH) ** 0.25
BETA_INIT = (8.0 * DEPTH) ** -0.25

OFF_QKV = 3 * DN_WIDTH
OFF_Z = OFF_QKV + DN_WIDTH
OFF_A = OFF_Z + DN_HEADS
OFF_B = OFF_A + DN_HEADS
OFF_POOL = OFF_B + POOL_WIDTH
OFF_GDN = OFF_POOL + D_MODEL
IN_COLS = OFF_GDN + D_MODEL
IN_SPLITS = (OFF_QKV, OFF_Z, OFF_A, OFF_B, OFF_POOL, OFF_GDN)

kernel_name = "hybrid_deltanet_pool_macaron_deepnorm"


def _layernorm(x, g, b):
    xf = x.astype(jnp.float32)
    mu = jnp.mean(xf, axis=-1, keepdims=True)
    var = jnp.mean(jnp.square(xf - mu), axis=-1, keepdims=True)
    y = (xf - mu) * lax.rsqrt(var + LN_EPS) * g.astype(jnp.float32) + b.astype(jnp.float32)
    return y.astype(x.dtype)


def _swiglu(x, w_gate, w_up, w_down):
    return (jax.nn.silu(x @ w_gate) * (x @ w_up)) @ w_down


def _causal_dwconv(x, w):
    c = x.shape[-1]
    return lax.conv_general_dilated(
        x, w[:, None, :].astype(x.dtype), window_strides=(1,), padding=[(CONV_K - 1, 0)],
        dimension_numbers=("NWC", "WIO", "NWC"), feature_group_count=c)


def _l2norm(x):
    return x * lax.rsqrt(jnp.sum(jnp.square(x), axis=-1, keepdims=True) + L2_EPS)


def _gated_delta_rule(q, k, v, g, beta):
    B, L, H, D = q.shape
    n = L // CHUNK
    ch = lambda t: t.reshape(B, n, CHUNK, H, D).transpose(1, 0, 3, 2, 4)
    q, k, v = ch(q), ch(k), ch(v)
    g = g.reshape(B, n, CHUNK, H).transpose(1, 0, 3, 2)
    beta = beta.reshape(B, n, CHUNK, H).transpose(1, 0, 3, 2)

    gc = jnp.cumsum(g, axis=-1)
    idx = jnp.arange(CHUNK)
    incl = idx[:, None] >= idx[None, :]
    strict = idx[:, None] > idx[None, :]
    diff = gc[..., :, None] - gc[..., None, :]
    decay = jnp.exp(jnp.where(incl, diff, -jnp.inf))

    kb = k * beta[..., None]
    a_low = jnp.where(strict, jnp.einsum("nbhcd,nbhed->nbhce", kb, k) * decay, 0.0)
    eye = jnp.eye(CHUNK, dtype=jnp.float32)
    rhs = jnp.concatenate([v * beta[..., None], kb * jnp.exp(gc)[..., None]], axis=-1)
    sol = lax.linalg.triangular_solve(a_low + eye, rhs, left_side=True, lower=True,
                                      unit_diagonal=True)
    u, w = sol[..., :D], sol[..., D:]

    attn = jnp.einsum("nbhcd,nbhed->nbhce", q, k) * decay
    q_dec = q * jnp.exp(gc)[..., None]
    g_last = gc[..., -1]
    k_dec = k * jnp.exp(g_last[..., None] - gc)[..., None]

    def step(state, inp):
        qd, kd, ui, wi, ai, gl = inp
        v_new = ui - jnp.einsum("bhcd,bhde->bhce", wi, state)
        o = jnp.einsum("bhcd,bhde->bhce", qd, state) + jnp.einsum("bhce,bhef->bhcf", ai, v_new)
        state = state * jnp.exp(gl)[..., None, None] + jnp.einsum("bhcd,bhce->bhde", kd, v_new)
        return state, o

    s0 = jnp.zeros((B, H, D, D), jnp.float32)
    _, o = lax.scan(step, s0, (q_dec, k_dec, u, w, attn, g_last))
    return o.transpose(1, 0, 3, 2, 4).reshape(B, L, H, D)


def _multiscale_pool(p, pool_w, pool_scale):
    B, L, _ = p.shape
    pf = p.astype(jnp.float32).reshape(B, L, POOL_GROUPS, POOL_GROUP_DIM)
    csum = jnp.cumsum(pf, axis=1)
    t = (jnp.arange(L, dtype=jnp.int32) + 1).astype(jnp.float32)
    means = []
    for gi, win in enumerate(POOL_WINDOWS):
        c = csum[:, :, gi]
        lag = jnp.pad(c, ((0, 0), (win, 0), (0, 0)))[:, :L]
        cnt = jnp.minimum(t, float(win))[None, :, None]
        means.append((c - lag) / cnt)
    mixed = jnp.stack(means, axis=2) - pf
    y = jnp.einsum("blgc,gcd->blgd", mixed, pool_w.astype(jnp.float32)).reshape(B, L, POOL_WIDTH)
    return (y * pool_scale.astype(jnp.float32)).astype(p.dtype)


def _hybrid_mixer(h, w_in, conv_w, a_log, dt_bias, dn_norm_w, w_dn_branch,
                  pool_w, pool_scale, w_pool_branch, w_mix_out):
    B, L, _ = h.shape
    proj = h @ w_in
    qkv, z, a, b, p, gate_dn, gate_pool = jnp.split(proj, IN_SPLITS, axis=-1)

    qkv = jax.nn.silu(_causal_dwconv(qkv, conv_w)).astype(jnp.float32)
    q, k, v = jnp.split(qkv, 3, axis=-1)
    hs = (B, L, DN_HEADS, DN_HEAD_DIM)
    q = _l2norm(q.reshape(hs)) * (DN_HEAD_DIM ** -0.5)
    k = _l2norm(k.reshape(hs))
    v = v.reshape(hs)
    beta = jax.nn.sigmoid(b.astype(jnp.float32))
    g = -jnp.exp(a_log.astype(jnp.float32)) * jax.nn.softplus(
        a.astype(jnp.float32) + dt_bias.astype(jnp.float32))
    o = _gated_delta_rule(q, k, v, g, beta)
    o = o * lax.rsqrt(jnp.mean(jnp.square(o), axis=-1, keepdims=True) + RMS_EPS)
    o = o * dn_norm_w.astype(jnp.float32) * jax.nn.silu(z.astype(jnp.float32).reshape(hs))
    y_dn = o.reshape(B, L, DN_WIDTH).astype(h.dtype) @ w_dn_branch

    y_pool = _multiscale_pool(p, pool_w, pool_scale) @ w_pool_branch

    merged = jax.nn.sigmoid(gate_dn) * y_dn + jax.nn.sigmoid(gate_pool) * y_pool
    return merged @ w_mix_out


def _cross_attention(h, m, wq, wk, wv, wo):
    B, L, _ = h.shape
    q = (h @ wq).reshape(B, L, XA_HEADS, XA_HEAD_DIM)
    k = (m @ wk).reshape(B, N_MEM, XA_HEADS, XA_HEAD_DIM)
    v = (m @ wv).reshape(B, N_MEM, XA_HEADS, XA_HEAD_DIM)
    s = jnp.einsum("bqhd,bkhd->bhqk", q, k).astype(jnp.float32) * (XA_HEAD_DIM ** -0.5)
    pr = jax.nn.softmax(s, axis=-1).astype(h.dtype)
    o = jnp.einsum("bhqk,bkhd->bqhd", pr, v).reshape(B, L, D_MODEL)
    return o @ wo


def setup_inputs(seed: int = 0) -> dict:
    key = jax.random.key(seed)
    ks = iter(jax.random.split(key, 40))
    f32 = jnp.float32

    def nrm(shape, scale):
        return jax.random.normal(next(ks), shape, f32) * scale

    def gain(shape):
        return 1.0 + 0.02 * jax.random.normal(next(ks), shape, f32)

    def bias(shape):
        return 0.02 * jax.random.normal(next(ks), shape, f32)

    Dp = DEPTH
    d = D_MODEL
    x = jax.random.normal(next(ks), (BATCH, SEQ, d), f32)
    mem = jax.random.normal(next(ks), (BATCH, N_MEM, d), f32)

    ffn1_w_gate = nrm((Dp, d, D_FF), d ** -0.5)
    ffn1_w_up = nrm((Dp, d, D_FF), d ** -0.5)
    ffn1_w_down = nrm((Dp, D_FF, d), BETA_INIT * D_FF ** -0.5)
    ln1_g, ln1_b = gain((Dp, d)), bias((Dp, d))

    w_in = nrm((Dp, d, IN_COLS), d ** -0.5)
    conv_w = nrm((Dp, CONV_K, 3 * DN_WIDTH), CONV_K ** -0.5)
    a_log = jnp.log(jax.random.uniform(next(ks), (Dp, DN_HEADS), f32, 1.0, 16.0))
    dt = jnp.exp(jax.random.uniform(next(ks), (Dp, DN_HEADS), f32,
                                    float(np.log(1e-3)), float(np.log(1e-1))))
    dt_bias = dt + jnp.log(-jnp.expm1(-dt))
    dn_norm_w = gain((Dp, DN_HEAD_DIM))
    w_dn_branch = nrm((Dp, DN_WIDTH, d), DN_WIDTH ** -0.5)
    pool_w = nrm((Dp, POOL_GROUPS, POOL_GROUP_DIM, POOL_GROUP_DIM), POOL_GROUP_DIM ** -0.5)
    pool_scale = gain((Dp, POOL_WIDTH))
    w_pool_branch = nrm((Dp, POOL_WIDTH, d), POOL_WIDTH ** -0.5)
    w_mix_out = nrm((Dp, d, d), BETA_INIT * d ** -0.5)
    ln2_g, ln2_b = gain((Dp, d)), bias((Dp, d))

    mem_ln_g, mem_ln_b = gain((Dp, d)), bias((Dp, d))
    xa_wq = nrm((Dp, d, d), d ** -0.5)
    xa_wk = nrm((Dp, d, d), d ** -0.5)
    xa_wv = nrm((Dp, d, d), d ** -0.5)
    xa_wo = nrm((Dp, d, d), BETA_INIT * d ** -0.5)
    ln3_g, ln3_b = gain((Dp, d)), bias((Dp, d))

    ffn2_w_gate = nrm((Dp, d, D_FF), d ** -0.5)
    ffn2_w_up = nrm((Dp, d, D_FF), d ** -0.5)
    ffn2_w_down = nrm((Dp, D_FF, d), BETA_INIT * D_FF ** -0.5)
    ln4_g, ln4_b = gain((Dp, d)), bias((Dp, d))

    return {"x": x, "mem": mem,
            "ffn1_w_gate": ffn1_w_gate, "ffn1_w_up": ffn1_w_up, "ffn1_w_down": ffn1_w_down,
            "ln1_g": ln1_g, "ln1_b": ln1_b,
            "w_in": w_in, "conv_w": conv_w, "a_log": a_log, "dt_bias": dt_bias,
            "dn_norm_w": dn_norm_w, "w_dn_branch": w_dn_branch,
            "pool_w": pool_w, "pool_scale": pool_scale, "w_pool_branch": w_pool_branch,
            "w_mix_out": w_mix_out, "ln2_g": ln2_g, "ln2_b": ln2_b,
            "mem_ln_g": mem_ln_g, "mem_ln_b": mem_ln_b,
            "xa_wq": xa_wq, "xa_wk": xa_wk, "xa_wv": xa_wv, "xa_wo": xa_wo,
            "ln3_g": ln3_g, "ln3_b": ln3_b,
            "ffn2_w_gate": ffn2_w_gate, "ffn2_w_up": ffn2_w_up, "ffn2_w_down": ffn2_w_down,
            "ln4_g": ln4_g, "ln4_b": ln4_b}


def reference(x, mem, ffn1_w_gate, ffn1_w_up, ffn1_w_down, ln1_g, ln1_b,
              w_in, conv_w, a_log, dt_bias, dn_norm_w, w_dn_branch,
              pool_w, pool_scale, w_pool_branch, w_mix_out, ln2_g, ln2_b,
              mem_ln_g, mem_ln_b, xa_wq, xa_wk, xa_wv, xa_wo, ln3_g, ln3_b,
              ffn2_w_gate, ffn2_w_up, ffn2_w_down, ln4_g, ln4_b):
    for l in range(DEPTH):
        x = _layernorm(ALPHA * x + 0.5 * _swiglu(x, ffn1_w_gate[l], ffn1_w_up[l], ffn1_w_down[l]),
                       ln1_g[l], ln1_b[l])
        mix = _hybrid_mixer(x, w_in[l], conv_w[l], a_log[l], dt_bias[l], dn_norm_w[l],
                            w_dn_branch[l], pool_w[l], pool_scale[l], w_pool_branch[l], w_mix_out[l])
        x = _layernorm(ALPHA * x + mix, ln2_g[l], ln2_b[l])
        m = _layernorm(mem, mem_ln_g[l], mem_ln_b[l])
        xa = _cross_attention(x, m, xa_wq[l], xa_wk[l], xa_wv[l], xa_wo[l])
        x = _layernorm(ALPHA * x + xa, ln3_g[l], ln3_b[l])
        x = _layernorm(ALPHA * x + 0.5 * _swiglu(x, ffn2_w_gate[l], ffn2_w_up[l], ffn2_w_down[l]),
                       ln4_g[l], ln4_b[l])
    return x
```

```python
import functools

import jax
import jax.numpy as jnp
from jax import lax
from jax.experimental import pallas as pl
from jax.experimental.pallas import tpu as pltpu

F32 = jnp.float32
BF16 = jnp.bfloat16

DN_HEADS = 4
DN_HEAD_DIM = 128
DN_WIDTH = DN_HEADS * DN_HEAD_DIM
CONV_K = 4
POOL_WINDOWS = (2, 4, 8, 16)
POOL_GROUP_DIM = 128
POOL_WIDTH = len(POOL_WINDOWS) * POOL_GROUP_DIM
XA_HEADS = 4
LN_EPS = 1e-5
RMS_EPS = 1e-6
L2_EPS = 1e-6
DEPTH = 1
ALPHA = (2.0 * DEPTH) ** 0.25

LANES = 128
HALO = 16
DELTA_CHUNK = 128
AB_PAD = LANES
VMEM_LIMIT = 56 * 1024 * 1024

TOKEN_TILE = 512
PREP_CHUNKS = 2
FF_SPLIT = 2


def _dot(a, b):
    return jnp.dot(a, b, preferred_element_type=F32)


def _dot_nt(a, b):
    return lax.dot_general(a, b, (((1,), (1,)), ((), ())), preferred_element_type=F32)


def _layernorm(y, g, b):
    mu = jnp.mean(y, axis=-1, keepdims=True)
    yc = y - mu
    var = jnp.mean(yc * yc, axis=-1, keepdims=True)
    return yc * lax.rsqrt(var + LN_EPS) * g + b


def _silu(x):
    return x * jax.nn.sigmoid(x)


def _const_spec(shape):
    nd = len(shape)
    return pl.BlockSpec(shape, lambda *_: (0,) * nd)


def _params(*sem):
    return pltpu.CompilerParams(dimension_semantics=sem, vmem_limit_bytes=VMEM_LIMIT)


def _ffn_ln_kernel(x_ref, wg_ref, wu_ref, wd_ref, g_ref, b_ref, o_ref):
    x = x_ref[...]
    xb = x.astype(BF16)
    ff = wg_ref.shape[1]
    step = ff // FF_SPLIT
    y = None
    for c in range(FF_SPLIT):
        cs = slice(c * step, (c + 1) * step)
        hg = _dot(xb, wg_ref[:, cs])
        hu = _dot(xb, wu_ref[:, cs])
        part = _dot((_silu(hg) * hu).astype(BF16), wd_ref[cs, :])
        y = part if y is None else y + part
    o_ref[...] = _layernorm(ALPHA * x + 0.5 * y, g_ref[...], b_ref[...])


def _ffn_ln(x2d, wg, wu, wd, g, b):
    t, d = x2d.shape
    ff = wg.shape[1]
    tm = min(TOKEN_TILE, t)
    row = pl.BlockSpec((tm, d), lambda i: (i, 0))
    return pl.pallas_call(
        _ffn_ln_kernel,
        grid=(t // tm,),
        in_specs=[row, _const_spec((d, ff)), _const_spec((d, ff)), _const_spec((ff, d)),
                  _const_spec((1, d)), _const_spec((1, d))],
        out_specs=row,
        out_shape=jax.ShapeDtypeStruct((t, d), F32),
        compiler_params=_params("parallel"),
        name="ffn_ln",
    )(x2d, wg, wu, wd, g, b)


def _mixer_proj_kernel(h_ref, halo_ref, wcat_ref, convw_ref, alog_ref, dtb_ref, poolw_ref,
                       pscale_ref, q_ref, k_ref, v_ref, gb_ref, yp_ref):
    i = pl.program_id(1)
    tm = h_ref.shape[1]
    qkv_w = 3 * DN_WIDTH
    halo = jnp.where(i == 0, 0.0, halo_ref[0])
    hcat = jnp.concatenate([halo, h_ref[0]], axis=0).astype(BF16)
    proj = _dot(hcat, wcat_ref[...])

    qkv_pre = proj[:, :qkv_w]
    conv = qkv_pre * convw_ref[CONV_K - 1:CONV_K, :]
    for s in range(1, CONV_K):
        conv = conv + pltpu.roll(qkv_pre, s, 0) * convw_ref[CONV_K - 1 - s:CONV_K - s, :]
    qkv = _silu(conv[HALO:, :])

    for part, ref, scale in ((0, q_ref, DN_HEAD_DIM ** -0.5), (1, k_ref, None), (2, v_ref, None)):
        for hh in range(DN_HEADS):
            lo = part * DN_WIDTH + hh * DN_HEAD_DIM
            seg = qkv[:, lo:lo + DN_HEAD_DIM]
            if part < 2:
                seg = seg * lax.rsqrt(jnp.sum(seg * seg, axis=-1, keepdims=True) + L2_EPS)
            if scale is not None:
                seg = seg * scale
            ref[0, :, hh * DN_HEAD_DIM:(hh + 1) * DN_HEAD_DIM] = seg

    ab = proj[HALO:, qkv_w + POOL_WIDTH:]
    xa = ab + dtb_ref[...]
    softplus = jnp.maximum(xa, 0.0) + jnp.log1p(jnp.exp(-jnp.abs(xa)))
    gval = -jnp.exp(alog_ref[...]) * softplus
    lane = lax.broadcasted_iota(jnp.int32, ab.shape, 1)
    gb_ref[0] = jnp.where(lane < DN_HEADS, gval, jax.nn.sigmoid(ab))

    t = (i * tm + lax.broadcasted_iota(jnp.int32, (tm, 1), 0) + 1).astype(F32)
    for gi, win in enumerate(POOL_WINDOWS):
        lo = qkv_w + gi * POOL_GROUP_DIM
        seg = proj[:, lo:lo + POOL_GROUP_DIM]
        wsum = seg
        sh = 1
        while sh < win:
            wsum = wsum + pltpu.roll(wsum, sh, 0)
            sh *= 2
        inv_cnt = 1.0 / jnp.minimum(t, float(win))
        mixed = wsum[HALO:, :] * inv_cnt - seg[HALO:, :]
        y = _dot(mixed.astype(BF16), poolw_ref[gi])
        cs = slice(gi * POOL_GROUP_DIM, (gi + 1) * POOL_GROUP_DIM)
        yp_ref[0, :, cs] = (y * pscale_ref[:, cs]).astype(BF16)


def _mixer_proj(h, wcat, conv_w, alog_pad, dtb_pad, pool_w, pool_scale):
    bsz, seq, d = h.shape
    tm = min(TOKEN_TILE, seq)
    hb = tm // HALO
    row = lambda w: pl.BlockSpec((1, tm, w), lambda b, i: (b, i, 0))
    out_f = jax.ShapeDtypeStruct((bsz, seq, DN_WIDTH), F32)
    return pl.pallas_call(
        _mixer_proj_kernel,
        grid=(bsz, seq // tm),
        in_specs=[row(d),
                  pl.BlockSpec((1, HALO, d), lambda b, i: (b, jnp.maximum(i * hb - 1, 0), 0)),
                  _const_spec(wcat.shape), _const_spec(conv_w.shape), _const_spec(alog_pad.shape),
                  _const_spec(dtb_pad.shape), _const_spec(pool_w.shape), _const_spec(pool_scale.shape)],
        out_specs=[row(DN_WIDTH), row(DN_WIDTH), row(DN_WIDTH), row(AB_PAD), row(POOL_WIDTH)],
        out_shape=[out_f, out_f, out_f,
                   jax.ShapeDtypeStruct((bsz, seq, AB_PAD), F32),
                   jax.ShapeDtypeStruct((bsz, seq, POOL_WIDTH), BF16)],
        compiler_params=_params("parallel", "parallel"),
        name="mixer_proj",
    )(h, h, wcat, conv_w, alog_pad, dtb_pad, pool_w, pool_scale)


def _delta_prep_kernel(q_ref, k_ref, v_ref, gb_ref, u_ref, w_ref, qd_ref, attn_ref, kdt_ref, egl_ref):
    c_len = DELTA_CHUNK
    n_chunks = q_ref.shape[1] // c_len
    ri = lax.broadcasted_iota(jnp.int32, (c_len, c_len), 0)
    ci = lax.broadcasted_iota(jnp.int32, (c_len, c_len), 1)
    incl = ri >= ci
    eye = ri == ci
    ltri = incl.astype(F32)
    level_masks = []
    s = 1
    while s < c_len:
        level_masks.append(((ri // (2 * s)) == (ci // (2 * s))) & ((ri // s) % 2 == 1) & ((ci // s) % 2 == 0))
        s *= 2
    lane = lax.broadcasted_iota(jnp.int32, (c_len, LANES), 1)

    for c in range(n_chunks):
        rows = slice(c * c_len, (c + 1) * c_len)
        gbc = gb_ref[0, rows, :]
        g_only = jnp.where(lane < DN_HEADS, gbc, 0.0)
        gc = jnp.dot(ltri, g_only, precision=lax.Precision.HIGHEST,
                     preferred_element_type=F32)
        gct = gc.T
        e_gc = jnp.exp(gc)
        egl_ref[0, c] = jnp.broadcast_to(jnp.exp(gct[0:8, c_len - 1:c_len]), (8, LANES))

        for hh in range(DN_HEADS):
            cs = slice(hh * DN_HEAD_DIM, (hh + 1) * DN_HEAD_DIM)
            qh = q_ref[0, rows, cs]
            kh = k_ref[0, rows, cs]
            vh = v_ref[0, rows, cs]
            beta = gbc[:, DN_HEADS + hh:DN_HEADS + hh + 1]
            gcol = gc[:, hh:hh + 1]
            grow = gct[hh:hh + 1, :]
            glast = gct[hh:hh + 1, c_len - 1:c_len]
            decay = jnp.exp(jnp.where(incl, gcol - grow, -jnp.inf))

            kb = kh * beta
            khb = kh.astype(BF16)
            a_mat = jnp.where(eye, 0.0, _dot_nt(kb.astype(BF16), khb) * decay)

            dinv = jnp.where(eye, 1.0, 0.0) - jnp.where(level_masks[0], a_mat, 0.0)
            for m in level_masks[1:]:
                db = dinv.astype(BF16)
                left = _dot(db, jnp.where(m, a_mat, 0.0).astype(BF16))
                dinv = dinv - _dot(left.astype(BF16), db)
            t_low = jnp.where(eye, 0.0, dinv).astype(BF16)

            ecol = e_gc[:, hh:hh + 1]
            rhs = jnp.concatenate([vh * beta, kb * ecol], axis=1)
            sol = rhs + _dot(t_low, rhs.astype(BF16))
            u_ref[0, rows, cs] = sol[:, :DN_HEAD_DIM]
            w_ref[0, rows, cs] = sol[:, DN_HEAD_DIM:].astype(BF16)
            qd_ref[0, rows, cs] = (qh * ecol).astype(BF16)
            attn_ref[0, rows, cs] = (_dot_nt(qh.astype(BF16), khb) * decay).astype(BF16)
            kdt_ref[0, rows, cs] = (kh.T * jnp.exp(glast - grow)).astype(BF16)


def _delta_prep(q, k, v, gb):
    bsz, seq, width = q.shape
    tm = min(PREP_CHUNKS * DELTA_CHUNK, seq)
    row = lambda w: pl.BlockSpec((1, tm, w), lambda b, i: (b, i, 0))
    n_chunks = seq // DELTA_CHUNK
    big = lambda dt: jax.ShapeDtypeStruct((bsz, seq, width), dt)
    return pl.pallas_call(
        _delta_prep_kernel,
        grid=(bsz, seq // tm),
        in_specs=[row(width), row(width), row(width), row(AB_PAD)],
        out_specs=[row(width)] * 5 + [pl.BlockSpec((1, tm // DELTA_CHUNK, 8, LANES), lambda b, i: (b, i, 0, 0))],
        out_shape=[big(F32), big(BF16), big(BF16), big(BF16), big(BF16),
                   jax.ShapeDtypeStruct((bsz, n_chunks, 8, LANES), F32)],
        compiler_params=_params("parallel", "parallel"),
        name="delta_prep",
    )(q, k, v, gb)


def _delta_scan_kernel(u_ref, w_ref, qd_ref, attn_ref, kdt_ref, egl_ref, o_ref, state_ref):
    @pl.when(pl.program_id(0) == 0)
    def _():
        state_ref[...] = jnp.zeros_like(state_ref)

    bsz = u_ref.shape[0]
    c_len = DELTA_CHUNK
    for b in range(bsz):
        for hh in range(DN_HEADS):
            idx = b * DN_HEADS + hh
            cs = slice(hh * DN_HEAD_DIM, (hh + 1) * DN_HEAD_DIM)
            state = state_ref[idx]
            wq = jnp.concatenate([w_ref[b, :, cs], qd_ref[b, :, cs]], axis=0)
            proj = _dot(wq, state.astype(BF16))
            v_new = (u_ref[b, :, cs] - proj[:c_len]).astype(BF16)
            o_ref[b, :, cs] = proj[c_len:] + _dot(attn_ref[b, :, cs], v_new)
            state_ref[idx] = state * egl_ref[b, 0, hh:hh + 1, :] + _dot(kdt_ref[b, :, cs], v_new)


def _delta_scan(u, w, qd, attn, kdt, egl):
    bsz, seq, width = u.shape
    n_chunks = seq // DELTA_CHUNK
    blk = pl.BlockSpec((bsz, DELTA_CHUNK, width), lambda n: (0, n, 0))
    return pl.pallas_call(
        _delta_scan_kernel,
        grid=(n_chunks,),
        in_specs=[blk] * 5 + [pl.BlockSpec((bsz, 1, 8, LANES), lambda n: (0, n, 0, 0))],
        out_specs=blk,
        out_shape=jax.ShapeDtypeStruct((bsz, seq, width), F32),
        scratch_shapes=[pltpu.VMEM((bsz * DN_HEADS, DN_HEAD_DIM, DN_HEAD_DIM), F32)],
        compiler_params=_params("arbitrary"),
        name="delta_scan",
    )(u, w, qd, attn, kdt, egl)


def _mixer_out_kernel(h_ref, o_ref, yp_ref, wz_ref, wgd_ref, wgp_ref, wdn_ref, wpool_ref, wmix_ref,
                      dnw_ref, g_ref, b_ref, out_ref):
    h = h_ref[...]
    hb = h.astype(BF16)
    z = _dot(hb, wz_ref[...])
    o = o_ref[...]
    gated = []
    for hh in range(DN_HEADS):
        cs = slice(hh * DN_HEAD_DIM, (hh + 1) * DN_HEAD_DIM)
        oh = o[:, cs]
        oh = oh * lax.rsqrt(jnp.mean(oh * oh, axis=-1, keepdims=True) + RMS_EPS)
        gated.append((oh * dnw_ref[...] * _silu(z[:, cs])).astype(BF16))
    y_dn = _dot(jnp.concatenate(gated, axis=1), wdn_ref[...])
    y_pool = _dot(yp_ref[...], wpool_ref[...])
    merged = (jax.nn.sigmoid(_dot(hb, wgd_ref[...])) * y_dn
              + jax.nn.sigmoid(_dot(hb, wgp_ref[...])) * y_pool)
    mix = _dot(merged.astype(BF16), wmix_ref[...])
    out_ref[...] = _layernorm(ALPHA * h + mix, g_ref[...], b_ref[...])


def _mixer_out(h2d, o2d, yp2d, wz, wgd, wgp, wdn, wpool, wmix, dnw, g, b):
    t, d = h2d.shape
    tm = min(TOKEN_TILE, t)
    row = lambda w: pl.BlockSpec((tm, w), lambda i: (i, 0))
    consts = (wz, wgd, wgp, wdn, wpool, wmix, dnw, g, b)
    return pl.pallas_call(
        _mixer_out_kernel,
        grid=(t // tm,),
        in_specs=[row(d), row(DN_WIDTH), row(POOL_WIDTH)] + [_const_spec(c.shape) for c in consts],
        out_specs=row(d),
        out_shape=jax.ShapeDtypeStruct((t, d), F32),
        compiler_params=_params("parallel"),
        name="mixer_out",
    )(h2d, o2d, yp2d, *consts)


def _mem_kv_kernel(m_ref, g_ref, b_ref, wk_ref, wv_ref, k_ref, v_ref):
    m = _layernorm(m_ref[0], g_ref[...], b_ref[...]).astype(BF16)
    k_ref[0] = _dot(m, wk_ref[...]).astype(BF16)
    v_ref[0] = _dot(m, wv_ref[...]).astype(BF16)


def _mem_kv(mem, g, b, wk, wv):
    bsz, n_mem, d = mem.shape
    blk = pl.BlockSpec((1, n_mem, d), lambda i: (i, 0, 0))
    out = jax.ShapeDtypeStruct((bsz, n_mem, d), BF16)
    return pl.pallas_call(
        _mem_kv_kernel,
        grid=(bsz,),
        in_specs=[blk, _const_spec(g.shape), _const_spec(b.shape), _const_spec(wk.shape), _const_spec(wv.shape)],
        out_specs=[blk, blk],
        out_shape=[out, out],
        compiler_params=_params("parallel"),
        name="mem_kv",
    )(mem, g, b, wk, wv)


def _xattn_ln_kernel(x_ref, k_ref, v_ref, wq_ref, wo_ref, g_ref, b_ref, o_ref):
    x = x_ref[0]
    d = x.shape[-1]
    hd = d // XA_HEADS
    q = _dot(x.astype(BF16), wq_ref[...])
    heads = []
    for hh in range(XA_HEADS):
        cs = slice(hh * hd, (hh + 1) * hd)
        s = _dot_nt(q[:, cs].astype(BF16), k_ref[0, :, cs]) * (hd ** -0.5)
        e = jnp.exp(s - jnp.max(s, axis=-1, keepdims=True))
        pr = e * (1.0 / jnp.sum(e, axis=-1, keepdims=True))
        heads.append(_dot(pr.astype(BF16), v_ref[0, :, cs]).astype(BF16))
    xa = _dot(jnp.concatenate(heads, axis=1), wo_ref[...])
    o_ref[0] = _layernorm(ALPHA * x + xa, g_ref[...], b_ref[...])


def _xattn_ln(x, k, v, wq, wo, g, b):
    bsz, seq, d = x.shape
    n_mem = k.shape[1]
    tm = min(TOKEN_TILE, seq)
    row = pl.BlockSpec((1, tm, d), lambda bb, i: (bb, i, 0))
    kv = pl.BlockSpec((1, n_mem, d), lambda bb, i: (bb, 0, 0))
    return pl.pallas_call(
        _xattn_ln_kernel,
        grid=(bsz, seq // tm),
        in_specs=[row, kv, kv, _const_spec(wq.shape), _const_spec(wo.shape),
                  _const_spec(g.shape), _const_spec(b.shape)],
        out_specs=row,
        out_shape=jax.ShapeDtypeStruct((bsz, seq, d), F32),
        compiler_params=_params("parallel", "parallel"),
        name="xattn_ln",
    )(x, k, v, wq, wo, g, b)


def _row(vec):
    return vec.reshape(1, -1).astype(F32)


def _pad_lanes(vec):
    return jnp.zeros((1, AB_PAD), F32).at[0, :vec.shape[0]].set(vec.astype(F32))


def kernel(x, mem, ffn1_w_gate, ffn1_w_up, ffn1_w_down, ln1_g, ln1_b, w_in, conv_w, a_log, dt_bias, dn_norm_w, w_dn_branch, pool_w, pool_scale, w_pool_branch, w_mix_out, ln2_g, ln2_b, mem_ln_g, mem_ln_b, xa_wq, xa_wk, xa_wv, xa_wo, ln3_g, ln3_b, ffn2_w_gate, ffn2_w_up, ffn2_w_down, ln4_g, ln4_b):
    bsz, seq, d = x.shape
    t = bsz * seq
    bf = lambda w: w.astype(BF16)
    o_qkv = 3 * DN_WIDTH
    o_z = o_qkv + DN_WIDTH
    o_ab = o_z + 2 * DN_HEADS
    o_pool = o_ab + POOL_WIDTH
    o_gdn = o_pool + d

    for l in range(ffn1_w_gate.shape[0]):
        x2d = _ffn_ln(x.reshape(t, d), bf(ffn1_w_gate[l]), bf(ffn1_w_up[l]), bf(ffn1_w_down[l]),
                      _row(ln1_g[l]), _row(ln1_b[l]))

        wi = w_in[l]
        w_ab = jnp.zeros((d, AB_PAD), F32).at[:, :2 * DN_HEADS].set(wi[:, o_z:o_ab])
        wcat = bf(jnp.concatenate([wi[:, :o_qkv], wi[:, o_ab:o_pool], w_ab], axis=1))
        q, k, v, gb, ypool = _mixer_proj(x2d.reshape(bsz, seq, d), wcat, conv_w[l].astype(F32),
                                         _pad_lanes(a_log[l]), _pad_lanes(dt_bias[l]),
                                         bf(pool_w[l]), _row(pool_scale[l]))
        u, w, qd, attn, kdt, egl = _delta_prep(q, k, v, gb)
        o = _delta_scan(u, w, qd, attn, kdt, egl)
        x2d = _mixer_out(x2d, o.reshape(t, DN_WIDTH), ypool.reshape(t, POOL_WIDTH),
                         bf(wi[:, o_qkv:o_z]), bf(wi[:, o_pool:o_gdn]), bf(wi[:, o_gdn:]),
                         bf(w_dn_branch[l]), bf(w_pool_branch[l]), bf(w_mix_out[l]),
                         _row(dn_norm_w[l]), _row(ln2_g[l]), _row(ln2_b[l]))

        mk, mv = _mem_kv(mem, _row(mem_ln_g[l]), _row(mem_ln_b[l]), bf(xa_wk[l]), bf(xa_wv[l]))
        x3 = _xattn_ln(x2d.reshape(bsz, seq, d), mk, mv, bf(xa_wq[l]), bf(xa_wo[l]),
                       _row(ln3_g[l]), _row(ln3_b[l]))
        x = _ffn_ln(x3.reshape(t, d), bf(ffn2_w_gate[l]), bf(ffn2_w_up[l]), bf(ffn2_w_down[l]),
                    _row(ln4_g[l]), _row(ln4_b[l])).reshape(bsz, seq, d)
    return x
```

```python
import functools

import jax
import jax.numpy as jnp
from jax import lax
from jax.experimental import pallas as pl
from jax.experimental.pallas import tpu as pltpu

F32 = jnp.float32
BF16 = jnp.bfloat16

DN_HEADS = 4
DN_HEAD_DIM = 128
DN_WIDTH = DN_HEADS * DN_HEAD_DIM
CONV_K = 4
POOL_WINDOWS = (2, 4, 8, 16)
POOL_GROUP_DIM = 128
POOL_WIDTH = len(POOL_WINDOWS) * POOL_GROUP_DIM
XA_HEADS = 4
LN_EPS = 1e-5
RMS_EPS = 1e-6
L2_EPS = 1e-6
DEPTH = 1
ALPHA = (2.0 * DEPTH) ** 0.25

LANES = 128
HALO = 16
DELTA_CHUNK = 128
AB_PAD = LANES
VMEM_LIMIT = 56 * 1024 * 1024

TOKEN_TILE = 512
PREP_CHUNKS = 2
FF_SPLIT = 2


def _dot(a, b):
    return jnp.dot(a, b, preferred_element_type=F32)


def _dot_nt(a, b):
    return lax.dot_general(a, b, (((1,), (1,)), ((), ())), preferred_element_type=F32)


def _layernorm(y, g, b):
    mu = jnp.mean(y, axis=-1, keepdims=True)
    yc = y - mu
    var = jnp.mean(yc * yc, axis=-1, keepdims=True)
    return yc * lax.rsqrt(var + LN_EPS) * g + b


def _silu(x):
    return x * jax.nn.sigmoid(x)


def _const_spec(shape):
    nd = len(shape)
    return pl.BlockSpec(shape, lambda *_: (0,) * nd)


def _params(*sem):
    return pltpu.CompilerParams(dimension_semantics=sem, vmem_limit_bytes=VMEM_LIMIT)


def _ffn_ln_kernel(x_ref, wg_ref, wu_ref, wd_ref, g_ref, b_ref, o_ref):
    x = x_ref[...]
    xb = x.astype(BF16)
    ff = wg_ref.shape[1]
    step = ff // FF_SPLIT
    y = None
    for c in range(FF_SPLIT):
        cs = slice(c * step, (c + 1) * step)
        hg = _dot(xb, wg_ref[:, cs])
        hu = _dot(xb, wu_ref[:, cs])
        part = _dot((_silu(hg) * hu).astype(BF16), wd_ref[cs, :])
        y = part if y is None else y + part
    o_ref[...] = _layernorm(ALPHA * x + 0.5 * y, g_ref[...], b_ref[...])


def _ffn_ln(x2d, wg, wu, wd, g, b):
    t, d = x2d.shape
    ff = wg.shape[1]
    tm = min(TOKEN_TILE, t)
    row = pl.BlockSpec((tm, d), lambda i: (i, 0))
    return pl.pallas_call(
        _ffn_ln_kernel,
        grid=(t // tm,),
        in_specs=[row, _const_spec((d, ff)), _const_spec((d, ff)), _const_spec((ff, d)),
                  _const_spec((1, d)), _const_spec((1, d))],
        out_specs=row,
        out_shape=jax.ShapeDtypeStruct((t, d), F32),
        compiler_params=_params("parallel"),
        name="ffn_ln",
    )(x2d, wg, wu, wd, g, b)


def _mixer_proj_kernel(h_ref, halo_ref, wcat_ref, convw_ref, alog_ref, dtb_ref, poolw_ref,
                       pscale_ref, q_ref, k_ref, v_ref, gb_ref, yp_ref):
    i = pl.program_id(1)
    tm = h_ref.shape[1]
    qkv_w = 3 * DN_WIDTH
    halo = jnp.where(i == 0, 0.0, halo_ref[0])
    hcat = jnp.concatenate([halo, h_ref[0]], axis=0).astype(BF16)
    proj = _dot(hcat, wcat_ref[...])

    qkv_pre = proj[:, :qkv_w]
    conv = qkv_pre * convw_ref[CONV_K - 1:CONV_K, :]
    for s in range(1, CONV_K):
        conv = conv + pltpu.roll(qkv_pre, s, 0) * convw_ref[CONV_K - 1 - s:CONV_K - s, :]
    qkv = _silu(conv[HALO:, :])

    for part, ref, scale in ((0, q_ref, DN_HEAD_DIM ** -0.5), (1, k_ref, None), (2, v_ref, None)):
        for hh in range(DN_HEADS):
            lo = part * DN_WIDTH + hh * DN_HEAD_DIM
            seg = qkv[:, lo:lo + DN_HEAD_DIM]
            if part < 2:
                seg = seg * lax.rsqrt(jnp.sum(seg * seg, axis=-1, keepdims=True) + L2_EPS)
            if scale is not None:
                seg = seg * scale
            ref[0, :, hh * DN_HEAD_DIM:(hh + 1) * DN_HEAD_DIM] = seg

    ab = proj[HALO:, qkv_w + POOL_WIDTH:]
    xa = ab + dtb_ref[...]
    softplus = jnp.maximum(xa, 0.0) + jnp.log1p(jnp.exp(-jnp.abs(xa)))
    gval = -jnp.exp(alog_ref[...]) * softplus
    lane = lax.broadcasted_iota(jnp.int32, ab.shape, 1)
    gb_ref[0] = jnp.where(lane < DN_HEADS, gval, jax.nn.sigmoid(ab))

    t = (i * tm + lax.broadcasted_iota(jnp.int32, (tm, 1), 0) + 1).astype(F32)
    for gi, win in enumerate(POOL_WINDOWS):
        lo = qkv_w + gi * POOL_GROUP_DIM
        seg = proj[:, lo:lo + POOL_GROUP_DIM]
        wsum = seg
        sh = 1
        while sh < win:
            wsum = wsum + pltpu.roll(wsum, sh, 0)
            sh *= 2
        inv_cnt = 1.0 / jnp.minimum(t, float(win))
        mixed = wsum[HALO:, :] * inv_cnt - seg[HALO:, :]
        y = _dot(mixed.astype(BF16), poolw_ref[gi])
        cs = slice(gi * POOL_GROUP_DIM, (gi + 1) * POOL_GROUP_DIM)
        yp_ref[0, :, cs] = (y * pscale_ref[:, cs]).astype(BF16)


def _mixer_proj(h, wcat, conv_w, alog_pad, dtb_pad, pool_w, pool_scale):
    bsz, seq, d = h.shape
    tm = min(TOKEN_TILE, seq)
    hb = tm // HALO
    row = lambda w: pl.BlockSpec((1, tm, w), lambda b, i: (b, i, 0))
    out_f = jax.ShapeDtypeStruct((bsz, seq, DN_WIDTH), F32)
    return pl.pallas_call(
        _mixer_proj_kernel,
        grid=(bsz, seq // tm),
        in_specs=[row(d),
                  pl.BlockSpec((1, HALO, d), lambda b, i: (b, jnp.maximum(i * hb - 1, 0), 0)),
                  _const_spec(wcat.shape), _const_spec(conv_w.shape), _const_spec(alog_pad.shape),
                  _const_spec(dtb_pad.shape), _const_spec(pool_w.shape), _const_spec(pool_scale.shape)],
        out_specs=[row(DN_WIDTH), row(DN_WIDTH), row(DN_WIDTH), row(AB_PAD), row(POOL_WIDTH)],
        out_shape=[out_f, out_f, out_f,
                   jax.ShapeDtypeStruct((bsz, seq, AB_PAD), F32),
                   jax.ShapeDtypeStruct((bsz, seq, POOL_WIDTH), BF16)],
        compiler_params=_params("parallel", "parallel"),
        name="mixer_proj",
    )(h, h, wcat, conv_w, alog_pad, dtb_pad, pool_w, pool_scale)


def _delta_prep_kernel(q_ref, k_ref, v_ref, gb_ref, u_ref, w_ref, qd_ref, attn_ref, kdt_ref, egl_ref):
    c_len = DELTA_CHUNK
    n_chunks = q_ref.shape[1] // c_len
    ri = lax.broadcasted_iota(jnp.int32, (c_len, c_len), 0)
    ci = lax.broadcasted_iota(jnp.int32, (c_len, c_len), 1)
    incl = ri >= ci
    eye = ri == ci
    ltri = incl.astype(F32)
    level_masks = []
    s = 1
    while s < c_len:
        level_masks.append(((ri // (2 * s)) == (ci // (2 * s))) & ((ri // s) % 2 == 1) & ((ci // s) % 2 == 0))
        s *= 2
    lane = lax.broadcasted_iota(jnp.int32, (c_len, LANES), 1)

    chains = []
    for c in range(n_chunks):
        rows = slice(c * c_len, (c + 1) * c_len)
        gbc = gb_ref[0, rows, :]
        g_only = jnp.where(lane < DN_HEADS, gbc, 0.0)
        gc = jnp.dot(ltri, g_only, precision=lax.Precision.HIGHEST,
                     preferred_element_type=F32)
        gct = gc.T
        e_gc = jnp.exp(gc)
        egl_ref[0, c] = jnp.broadcast_to(jnp.exp(gct[0:8, c_len - 1:c_len]), (8, LANES))
        for hh in range(DN_HEADS):
            cs = slice(hh * DN_HEAD_DIM, (hh + 1) * DN_HEAD_DIM)
            chains.append(dict(
                rows=rows, cs=cs,
                beta=gbc[:, DN_HEADS + hh:DN_HEADS + hh + 1],
                gcol=gc[:, hh:hh + 1],
                grow=gct[hh:hh + 1, :],
                glast=gct[hh:hh + 1, c_len - 1:c_len],
                ecol=e_gc[:, hh:hh + 1]))

    for ch in chains:
        kh = k_ref[0, ch["rows"], ch["cs"]]
        ch["decay"] = jnp.exp(jnp.where(incl, ch["gcol"] - ch["grow"], -jnp.inf))
        ch["kb"] = kh * ch["beta"]
        ch["khb"] = kh.astype(BF16)
    for ch in chains:
        ch["a"] = jnp.where(eye, 0.0, _dot_nt(ch["kb"].astype(BF16), ch["khb"]) * ch["decay"])

    for ch in chains:
        ch["dinv"] = jnp.where(eye, 1.0, 0.0) - jnp.where(level_masks[0], ch["a"], 0.0)
    for m in level_masks[1:]:
        for ch in chains:
            ch["db"] = ch["dinv"].astype(BF16)
            ch["left"] = _dot(ch["db"], jnp.where(m, ch["a"], 0.0).astype(BF16))
        for ch in chains:
            ch["dinv"] = ch["dinv"] - _dot(ch["left"].astype(BF16), ch["db"])

    for ch in chains:
        rows, cs = ch["rows"], ch["cs"]
        t_low = jnp.where(eye, 0.0, ch["dinv"]).astype(BF16)
        rhs = jnp.concatenate([v_ref[0, rows, cs] * ch["beta"], ch["kb"] * ch["ecol"]], axis=1)
        sol = rhs + _dot(t_low, rhs.astype(BF16))
        u_ref[0, rows, cs] = sol[:, :DN_HEAD_DIM]
        w_ref[0, rows, cs] = sol[:, DN_HEAD_DIM:].astype(BF16)
    for ch in chains:
        rows, cs = ch["rows"], ch["cs"]
        qh = q_ref[0, rows, cs]
        qd_ref[0, rows, cs] = (qh * ch["ecol"]).astype(BF16)
        attn_ref[0, rows, cs] = (_dot_nt(qh.astype(BF16), ch["khb"]) * ch["decay"]).astype(BF16)
        kdt_ref[0, rows, cs] = (k_ref[0, rows, cs].T * jnp.exp(ch["glast"] - ch["grow"])).astype(BF16)


def _delta_prep(q, k, v, gb):
    bsz, seq, width = q.shape
    tm = min(PREP_CHUNKS * DELTA_CHUNK, seq)
    row = lambda w: pl.BlockSpec((1, tm, w), lambda b, i: (b, i, 0))
    n_chunks = seq // DELTA_CHUNK
    big = lambda dt: jax.ShapeDtypeStruct((bsz, seq, width), dt)
    return pl.pallas_call(
        _delta_prep_kernel,
        grid=(bsz, seq // tm),
        in_specs=[row(width), row(width), row(width), row(AB_PAD)],
        out_specs=[row(width)] * 5 + [pl.BlockSpec((1, tm // DELTA_CHUNK, 8, LANES), lambda b, i: (b, i, 0, 0))],
        out_shape=[big(F32), big(BF16), big(BF16), big(BF16), big(BF16),
                   jax.ShapeDtypeStruct((bsz, n_chunks, 8, LANES), F32)],
        compiler_params=_params("parallel", "parallel"),
        name="delta_prep",
    )(q, k, v, gb)


def _delta_scan_kernel(u_ref, w_ref, qd_ref, attn_ref, kdt_ref, egl_ref, o_ref, state_ref):
    @pl.when(pl.program_id(0) == 0)
    def _():
        state_ref[...] = jnp.zeros_like(state_ref)

    bsz = u_ref.shape[0]
    c_len = DELTA_CHUNK
    chains = [(b, hh, slice(hh * DN_HEAD_DIM, (hh + 1) * DN_HEAD_DIM))
              for b in range(bsz) for hh in range(DN_HEADS)]
    proj = [_dot(jnp.concatenate([w_ref[b, :, cs], qd_ref[b, :, cs]], axis=0),
                 state_ref[b * DN_HEADS + hh].astype(BF16))
            for b, hh, cs in chains]
    v_new = [(u_ref[b, :, cs] - p[:c_len]).astype(BF16) for (b, hh, cs), p in zip(chains, proj)]
    for (b, hh, cs), p, vn in zip(chains, proj, v_new):
        o_ref[b, :, cs] = p[c_len:] + _dot(attn_ref[b, :, cs], vn)
    for (b, hh, cs), vn in zip(chains, v_new):
        idx = b * DN_HEADS + hh
        state_ref[idx] = state_ref[idx] * egl_ref[b, 0, hh:hh + 1, :] + _dot(kdt_ref[b, :, cs], vn)


def _delta_scan(u, w, qd, attn, kdt, egl):
    bsz, seq, width = u.shape
    n_chunks = seq // DELTA_CHUNK
    blk = pl.BlockSpec((bsz, DELTA_CHUNK, width), lambda n: (0, n, 0))
    return pl.pallas_call(
        _delta_scan_kernel,
        grid=(n_chunks,),
        in_specs=[blk] * 5 + [pl.BlockSpec((bsz, 1, 8, LANES), lambda n: (0, n, 0, 0))],
        out_specs=blk,
        out_shape=jax.ShapeDtypeStruct((bsz, seq, width), F32),
        scratch_shapes=[pltpu.VMEM((bsz * DN_HEADS, DN_HEAD_DIM, DN_HEAD_DIM), F32)],
        compiler_params=_params("arbitrary"),
        name="delta_scan",
    )(u, w, qd, attn, kdt, egl)


def _mixer_out_kernel(h_ref, o_ref, yp_ref, wz_ref, wgd_ref, wgp_ref, wdn_ref, wpool_ref, wmix_ref,
                      dnw_ref, g_ref, b_ref, out_ref):
    h = h_ref[...]
    hb = h.astype(BF16)
    z = _dot(hb, wz_ref[...])
    o = o_ref[...]
    gated = []
    for hh in range(DN_HEADS):
        cs = slice(hh * DN_HEAD_DIM, (hh + 1) * DN_HEAD_DIM)
        oh = o[:, cs]
        oh = oh * lax.rsqrt(jnp.mean(oh * oh, axis=-1, keepdims=True) + RMS_EPS)
        gated.append((oh * dnw_ref[...] * _silu(z[:, cs])).astype(BF16))
    y_dn = _dot(jnp.concatenate(gated, axis=1), wdn_ref[...])
    y_pool = _dot(yp_ref[...], wpool_ref[...])
    merged = (jax.nn.sigmoid(_dot(hb, wgd_ref[...])) * y_dn
              + jax.nn.sigmoid(_dot(hb, wgp_ref[...])) * y_pool)
    mix = _dot(merged.astype(BF16), wmix_ref[...])
    out_ref[...] = _layernorm(ALPHA * h + mix, g_ref[...], b_ref[...])


def _mixer_out(h2d, o2d, yp2d, wz, wgd, wgp, wdn, wpool, wmix, dnw, g, b):
    t, d = h2d.shape
    tm = min(TOKEN_TILE, t)
    row = lambda w: pl.BlockSpec((tm, w), lambda i: (i, 0))
    consts = (wz, wgd, wgp, wdn, wpool, wmix, dnw, g, b)
    return pl.pallas_call(
        _mixer_out_kernel,
        grid=(t // tm,),
        in_specs=[row(d), row(DN_WIDTH), row(POOL_WIDTH)] + [_const_spec(c.shape) for c in consts],
        out_specs=row(d),
        out_shape=jax.ShapeDtypeStruct((t, d), F32),
        compiler_params=_params("parallel"),
        name="mixer_out",
    )(h2d, o2d, yp2d, *consts)


def _mem_kv_kernel(m_ref, g_ref, b_ref, wk_ref, wv_ref, k_ref, v_ref):
    m = _layernorm(m_ref[0], g_ref[...], b_ref[...]).astype(BF16)
    k_ref[0] = _dot(m, wk_ref[...]).astype(BF16)
    v_ref[0] = _dot(m, wv_ref[...]).astype(BF16)


def _mem_kv(mem, g, b, wk, wv):
    bsz, n_mem, d = mem.shape
    blk = pl.BlockSpec((1, n_mem, d), lambda i: (i, 0, 0))
    out = jax.ShapeDtypeStruct((bsz, n_mem, d), BF16)
    return pl.pallas_call(
        _mem_kv_kernel,
        grid=(bsz,),
        in_specs=[blk, _const_spec(g.shape), _const_spec(b.shape), _const_spec(wk.shape), _const_spec(wv.shape)],
        out_specs=[blk, blk],
        out_shape=[out, out],
        compiler_params=_params("parallel"),
        name="mem_kv",
    )(mem, g, b, wk, wv)


def _xattn_ln_kernel(x_ref, k_ref, v_ref, wq_ref, wo_ref, g_ref, b_ref, o_ref):
    x = x_ref[0]
    d = x.shape[-1]
    hd = d // XA_HEADS
    q = _dot(x.astype(BF16), wq_ref[...])
    heads = []
    for hh in range(XA_HEADS):
        cs = slice(hh * hd, (hh + 1) * hd)
        s = _dot_nt(q[:, cs].astype(BF16), k_ref[0, :, cs]) * (hd ** -0.5)
        e = jnp.exp(s - jnp.max(s, axis=-1, keepdims=True))
        pr = e * (1.0 / jnp.sum(e, axis=-1, keepdims=True))
        heads.append(_dot(pr.astype(BF16), v_ref[0, :, cs]).astype(BF16))
    xa = _dot(jnp.concatenate(heads, axis=1), wo_ref[...])
    o_ref[0] = _layernorm(ALPHA * x + xa, g_ref[...], b_ref[...])


def _xattn_ln(x, k, v, wq, wo, g, b):
    bsz, seq, d = x.shape
    n_mem = k.shape[1]
    tm = min(TOKEN_TILE, seq)
    row = pl.BlockSpec((1, tm, d), lambda bb, i: (bb, i, 0))
    kv = pl.BlockSpec((1, n_mem, d), lambda bb, i: (bb, 0, 0))
    return pl.pallas_call(
        _xattn_ln_kernel,
        grid=(bsz, seq // tm),
        in_specs=[row, kv, kv, _const_spec(wq.shape), _const_spec(wo.shape),
                  _const_spec(g.shape), _const_spec(b.shape)],
        out_specs=row,
        out_shape=jax.ShapeDtypeStruct((bsz, seq, d), F32),
        compiler_params=_params("parallel", "parallel"),
        name="xattn_ln",
    )(x, k, v, wq, wo, g, b)


def _row(vec):
    return vec.reshape(1, -1).astype(F32)


def _pad_lanes(vec):
    return jnp.zeros((1, AB_PAD), F32).at[0, :vec.shape[0]].set(vec.astype(F32))


def kernel(x, mem, ffn1_w_gate, ffn1_w_up, ffn1_w_down, ln1_g, ln1_b, w_in, conv_w, a_log, dt_bias, dn_norm_w, w_dn_branch, pool_w, pool_scale, w_pool_branch, w_mix_out, ln2_g, ln2_b, mem_ln_g, mem_ln_b, xa_wq, xa_wk, xa_wv, xa_wo, ln3_g, ln3_b, ffn2_w_gate, ffn2_w_up, ffn2_w_down, ln4_g, ln4_b):
    bsz, seq, d = x.shape
    t = bsz * seq
    bf = lambda w: w.astype(BF16)
    o_qkv = 3 * DN_WIDTH
    o_z = o_qkv + DN_WIDTH
    o_ab = o_z + 2 * DN_HEADS
    o_pool = o_ab + POOL_WIDTH
    o_gdn = o_pool + d

    for l in range(ffn1_w_gate.shape[0]):
        x2d = _ffn_ln(x.reshape(t, d), bf(ffn1_w_gate[l]), bf(ffn1_w_up[l]), bf(ffn1_w_down[l]),
                      _row(ln1_g[l]), _row(ln1_b[l]))

        wi = w_in[l]
        w_ab = jnp.zeros((d, AB_PAD), F32).at[:, :2 * DN_HEADS].set(wi[:, o_z:o_ab])
        wcat = bf(jnp.concatenate([wi[:, :o_qkv], wi[:, o_ab:o_pool], w_ab], axis=1))
        q, k, v, gb, ypool = _mixer_proj(x2d.reshape(bsz, seq, d), wcat, conv_w[l].astype(F32),
                                         _pad_lanes(a_log[l]), _pad_lanes(dt_bias[l]),
                                         bf(pool_w[l]), _row(pool_scale[l]))
        u, w, qd, attn, kdt, egl = _delta_prep(q, k, v, gb)
        o = _delta_scan(u, w, qd, attn, kdt, egl)
        x2d = _mixer_out(x2d, o.reshape(t, DN_WIDTH), ypool.reshape(t, POOL_WIDTH),
                         bf(wi[:, o_qkv:o_z]), bf(wi[:, o_pool:o_gdn]), bf(wi[:, o_gdn:]),
                         bf(w_dn_branch[l]), bf(w_pool_branch[l]), bf(w_mix_out[l]),
                         _row(dn_norm_w[l]), _row(ln2_g[l]), _row(ln2_b[l]))

        mk, mv = _mem_kv(mem, _row(mem_ln_g[l]), _row(mem_ln_b[l]), bf(xa_wk[l]), bf(xa_wv[l]))
        x3 = _xattn_ln(x2d.reshape(bsz, seq, d), mk, mv, bf(xa_wq[l]), bf(xa_wo[l]),
                       _row(ln3_g[l]), _row(ln3_b[l]))
        x = _ffn_ln(x3.reshape(t, d), bf(ffn2_w_gate[l]), bf(ffn2_w_up[l]), bf(ffn2_w_down[l]),
                    _row(ln4_g[l]), _row(ln4_b[l])).reshape(bsz, seq, d)
    return x
```

```python
import functools

import jax
import jax.numpy as jnp
from jax import lax
from jax.experimental import pallas as pl
from jax.experimental.pallas import tpu as pltpu

F32 = jnp.float32
BF16 = jnp.bfloat16

DN_HEADS = 4
DN_HEAD_DIM = 128
DN_WIDTH = DN_HEADS * DN_HEAD_DIM
CONV_K = 4
POOL_WINDOWS = (2, 4, 8, 16)
POOL_GROUP_DIM = 128
POOL_WIDTH = len(POOL_WINDOWS) * POOL_GROUP_DIM
XA_HEADS = 4
LN_EPS = 1e-5
RMS_EPS = 1e-6
L2_EPS = 1e-6
DEPTH = 1
ALPHA = (2.0 * DEPTH) ** 0.25

LANES = 128
HALO = 16
DELTA_CHUNK = 128
AB_PAD = LANES
VMEM_LIMIT = 56 * 1024 * 1024

TOKEN_TILE = 512
PREP_CHUNKS = 4
MXU_DIM = 256
FF_SLICE = 6 * MXU_DIM


def _dot(a, b):
    return jnp.dot(a, b, preferred_element_type=F32)


def _dot_nt(a, b):
    return lax.dot_general(a, b, (((1,), (1,)), ((), ())), preferred_element_type=F32)


def _layernorm(y, g, b):
    mu = jnp.mean(y, axis=-1, keepdims=True)
    yc = y - mu
    var = jnp.mean(yc * yc, axis=-1, keepdims=True)
    return yc * lax.rsqrt(var + LN_EPS) * g + b


def _silu(x):
    return x * jax.nn.sigmoid(x)


def _const_spec(shape):
    nd = len(shape)
    return pl.BlockSpec(shape, lambda *_: (0,) * nd)


def _params(*sem):
    return pltpu.CompilerParams(dimension_semantics=sem, vmem_limit_bytes=VMEM_LIMIT)


def _ffn_ln_kernel(x_ref, wg_ref, wu_ref, wd_ref, g_ref, b_ref, o_ref):
    x = x_ref[...]
    xb = x.astype(BF16)
    ff = wg_ref.shape[1]
    y = None
    for lo in range(0, ff, FF_SLICE):
        cs = slice(lo, min(lo + FF_SLICE, ff))
        hg = _dot(xb, wg_ref[:, cs])
        hu = _dot(xb, wu_ref[:, cs])
        part = _dot((_silu(hg) * hu).astype(BF16), wd_ref[cs, :])
        y = part if y is None else y + part
    o_ref[...] = _layernorm(ALPHA * x + 0.5 * y, g_ref[...], b_ref[...])


def _ffn_ln(x2d, wg, wu, wd, g, b):
    t, d = x2d.shape
    ff = wg.shape[1]
    tm = min(TOKEN_TILE, t)
    row = pl.BlockSpec((tm, d), lambda i: (i, 0))
    return pl.pallas_call(
        _ffn_ln_kernel,
        grid=(t // tm,),
        in_specs=[row, _const_spec((d, ff)), _const_spec((d, ff)), _const_spec((ff, d)),
                  _const_spec((1, d)), _const_spec((1, d))],
        out_specs=row,
        out_shape=jax.ShapeDtypeStruct((t, d), F32),
        compiler_params=_params("parallel"),
        name="ffn_ln",
    )(x2d, wg, wu, wd, g, b)


def _mixer_proj_kernel(h_ref, halo_ref, wcat_ref, convw_ref, alog_ref, dtb_ref, poolw_ref,
                       pscale_ref, q_ref, k_ref, v_ref, gb_ref, yp_ref):
    i = pl.program_id(1)
    tm = h_ref.shape[1]
    qkv_w = 3 * DN_WIDTH
    halo = jnp.where(i == 0, 0.0, halo_ref[0])
    hcat = jnp.concatenate([halo, h_ref[0]], axis=0).astype(BF16)
    proj = _dot(hcat, wcat_ref[...])

    qkv_pre = proj[:, :qkv_w]
    conv = qkv_pre * convw_ref[CONV_K - 1:CONV_K, :]
    for s in range(1, CONV_K):
        conv = conv + pltpu.roll(qkv_pre, s, 0) * convw_ref[CONV_K - 1 - s:CONV_K - s, :]
    qkv = _silu(conv[HALO:, :])

    for part, ref, scale in ((0, q_ref, DN_HEAD_DIM ** -0.5), (1, k_ref, None), (2, v_ref, None)):
        for hh in range(DN_HEADS):
            lo = part * DN_WIDTH + hh * DN_HEAD_DIM
            seg = qkv[:, lo:lo + DN_HEAD_DIM]
            if part < 2:
                seg = seg * lax.rsqrt(jnp.sum(seg * seg, axis=-1, keepdims=True) + L2_EPS)
            if scale is not None:
                seg = seg * scale
            ref[0, :, hh * DN_HEAD_DIM:(hh + 1) * DN_HEAD_DIM] = seg

    ab = proj[HALO:, qkv_w + POOL_WIDTH:]
    xa = ab + dtb_ref[...]
    softplus = jnp.maximum(xa, 0.0) + jnp.log1p(jnp.exp(-jnp.abs(xa)))
    gval = -jnp.exp(alog_ref[...]) * softplus
    lane = lax.broadcasted_iota(jnp.int32, ab.shape, 1)
    gb_ref[0] = jnp.where(lane < DN_HEADS, gval, jax.nn.sigmoid(ab))

    t = (i * tm + lax.broadcasted_iota(jnp.int32, (tm, 1), 0) + 1).astype(F32)
    for gi, win in enumerate(POOL_WINDOWS):
        lo = qkv_w + gi * POOL_GROUP_DIM
        seg = proj[:, lo:lo + POOL_GROUP_DIM]
        wsum = seg
        sh = 1
        while sh < win:
            wsum = wsum + pltpu.roll(wsum, sh, 0)
            sh *= 2
        inv_cnt = 1.0 / jnp.minimum(t, float(win))
        mixed = wsum[HALO:, :] * inv_cnt - seg[HALO:, :]
        y = _dot(mixed.astype(BF16), poolw_ref[gi])
        cs = slice(gi * POOL_GROUP_DIM, (gi + 1) * POOL_GROUP_DIM)
        yp_ref[0, :, cs] = (y * pscale_ref[:, cs]).astype(BF16)


def _mixer_proj(h, wcat, conv_w, alog_pad, dtb_pad, pool_w, pool_scale):
    bsz, seq, d = h.shape
    tm = min(TOKEN_TILE, seq)
    hb = tm // HALO
    row = lambda w: pl.BlockSpec((1, tm, w), lambda b, i: (b, i, 0))
    out_f = jax.ShapeDtypeStruct((bsz, seq, DN_WIDTH), F32)
    return pl.pallas_call(
        _mixer_proj_kernel,
        grid=(bsz, seq // tm),
        in_specs=[row(d),
                  pl.BlockSpec((1, HALO, d), lambda b, i: (b, jnp.maximum(i * hb - 1, 0), 0)),
                  _const_spec(wcat.shape), _const_spec(conv_w.shape), _const_spec(alog_pad.shape),
                  _const_spec(dtb_pad.shape), _const_spec(pool_w.shape), _const_spec(pool_scale.shape)],
        out_specs=[row(DN_WIDTH), row(DN_WIDTH), row(DN_WIDTH), row(AB_PAD), row(POOL_WIDTH)],
        out_shape=[out_f, out_f, out_f,
                   jax.ShapeDtypeStruct((bsz, seq, AB_PAD), F32),
                   jax.ShapeDtypeStruct((bsz, seq, POOL_WIDTH), BF16)],
        compiler_params=_params("parallel", "parallel"),
        name="mixer_proj",
    )(h, h, wcat, conv_w, alog_pad, dtb_pad, pool_w, pool_scale)


def _delta_prep_kernel(q_ref, k_ref, v_ref, gb_ref, u_ref, w_ref, qd_ref, attn_ref, kdt_ref, egl_ref):
    c_len = DELTA_CHUNK
    n_chunks = q_ref.shape[1] // c_len
    ri = lax.broadcasted_iota(jnp.int32, (c_len, c_len), 0)
    ci = lax.broadcasted_iota(jnp.int32, (c_len, c_len), 1)
    incl = ri >= ci
    eye = ri == ci
    ltri = incl.astype(F32)
    level_masks = []
    s = 1
    while s < c_len:
        level_masks.append(((ri // (2 * s)) == (ci // (2 * s))) & ((ri // s) % 2 == 1) & ((ci // s) % 2 == 0))
        s *= 2
    lane = lax.broadcasted_iota(jnp.int32, (c_len, LANES), 1)

    chains = []
    for c in range(n_chunks):
        rows = slice(c * c_len, (c + 1) * c_len)
        gbc = gb_ref[0, rows, :]
        g_only = jnp.where(lane < DN_HEADS, gbc, 0.0)
        gc = jnp.dot(ltri, g_only, precision=lax.Precision.HIGHEST,
                     preferred_element_type=F32)
        gct = gc.T
        e_gc = jnp.exp(gc)
        egl_ref[0, c] = jnp.broadcast_to(jnp.exp(gct[0:8, c_len - 1:c_len]), (8, LANES))
        for hh in range(DN_HEADS):
            cs = slice(hh * DN_HEAD_DIM, (hh + 1) * DN_HEAD_DIM)
            chains.append(dict(
                rows=rows, cs=cs,
                beta=gbc[:, DN_HEADS + hh:DN_HEADS + hh + 1],
                gcol=gc[:, hh:hh + 1],
                grow=gct[hh:hh + 1, :],
                glast=gct[hh:hh + 1, c_len - 1:c_len],
                ecol=e_gc[:, hh:hh + 1]))

    for ch in chains:
        kh = k_ref[0, ch["rows"], ch["cs"]]
        ch["decay"] = jnp.exp(jnp.where(incl, ch["gcol"] - ch["grow"], -jnp.inf))
        ch["kb"] = kh * ch["beta"]
        ch["khb"] = kh.astype(BF16)
    for ch in chains:
        ch["a"] = jnp.where(eye, 0.0, _dot_nt(ch["kb"].astype(BF16), ch["khb"]) * ch["decay"])

    for ch in chains:
        ch["dinv"] = jnp.where(eye, 1.0, 0.0) - jnp.where(level_masks[0], ch["a"], 0.0)
    for m in level_masks[1:]:
        for ch in chains:
            ch["db"] = ch["dinv"].astype(BF16)
            ch["left"] = _dot(ch["db"], jnp.where(m, ch["a"], 0.0).astype(BF16))
        for ch in chains:
            ch["dinv"] = ch["dinv"] - _dot(ch["left"].astype(BF16), ch["db"])

    for ch in chains:
        rows, cs = ch["rows"], ch["cs"]
        t_low = jnp.where(eye, 0.0, ch["dinv"]).astype(BF16)
        rhs = jnp.concatenate([v_ref[0, rows, cs] * ch["beta"], ch["kb"] * ch["ecol"]], axis=1)
        sol = rhs + _dot(t_low, rhs.astype(BF16))
        u_ref[0, rows, cs] = sol[:, :DN_HEAD_DIM]
        w_ref[0, rows, cs] = sol[:, DN_HEAD_DIM:].astype(BF16)
    for ch in chains:
        rows, cs = ch["rows"], ch["cs"]
        qh = q_ref[0, rows, cs]
        qd_ref[0, rows, cs] = (qh * ch["ecol"]).astype(BF16)
        attn_ref[0, rows, cs] = (_dot_nt(qh.astype(BF16), ch["khb"]) * ch["decay"]).astype(BF16)
        kdt_ref[0, rows, cs] = (k_ref[0, rows, cs].T * jnp.exp(ch["glast"] - ch["grow"])).astype(BF16)


def _delta_prep(q, k, v, gb):
    bsz, seq, width = q.shape
    tm = min(PREP_CHUNKS * DELTA_CHUNK, seq)
    row = lambda w: pl.BlockSpec((1, tm, w), lambda b, i: (b, i, 0))
    n_chunks = seq // DELTA_CHUNK
    big = lambda dt: jax.ShapeDtypeStruct((bsz, seq, width), dt)
    return pl.pallas_call(
        _delta_prep_kernel,
        grid=(bsz, seq // tm),
        in_specs=[row(width), row(width), row(width), row(AB_PAD)],
        out_specs=[row(width)] * 5 + [pl.BlockSpec((1, tm // DELTA_CHUNK, 8, LANES), lambda b, i: (b, i, 0, 0))],
        out_shape=[big(F32), big(BF16), big(BF16), big(BF16), big(BF16),
                   jax.ShapeDtypeStruct((bsz, n_chunks, 8, LANES), F32)],
        compiler_params=_params("parallel", "parallel"),
        name="delta_prep",
    )(q, k, v, gb)


def _delta_scan_kernel(u_ref, w_ref, qd_ref, attn_ref, kdt_ref, egl_ref, o_ref, state_ref):
    @pl.when(pl.program_id(0) == 0)
    def _():
        state_ref[...] = jnp.zeros_like(state_ref)

    bsz = u_ref.shape[0]
    c_len = DELTA_CHUNK
    chains = [(b, hh, slice(hh * DN_HEAD_DIM, (hh + 1) * DN_HEAD_DIM))
              for b in range(bsz) for hh in range(DN_HEADS)]
    proj = [_dot(jnp.concatenate([w_ref[b, :, cs], qd_ref[b, :, cs]], axis=0),
                 state_ref[b * DN_HEADS + hh].astype(BF16))
            for b, hh, cs in chains]
    v_new = [(u_ref[b, :, cs] - p[:c_len]).astype(BF16) for (b, hh, cs), p in zip(chains, proj)]
    for (b, hh, cs), p, vn in zip(chains, proj, v_new):
        o_ref[b, :, cs] = p[c_len:] + _dot(attn_ref[b, :, cs], vn)
    for (b, hh, cs), vn in zip(chains, v_new):
        idx = b * DN_HEADS + hh
        state_ref[idx] = state_ref[idx] * egl_ref[b, 0, hh:hh + 1, :] + _dot(kdt_ref[b, :, cs], vn)


def _delta_scan(u, w, qd, attn, kdt, egl):
    bsz, seq, width = u.shape
    n_chunks = seq // DELTA_CHUNK
    blk = pl.BlockSpec((bsz, DELTA_CHUNK, width), lambda n: (0, n, 0))
    return pl.pallas_call(
        _delta_scan_kernel,
        grid=(n_chunks,),
        in_specs=[blk] * 5 + [pl.BlockSpec((bsz, 1, 8, LANES), lambda n: (0, n, 0, 0))],
        out_specs=blk,
        out_shape=jax.ShapeDtypeStruct((bsz, seq, width), F32),
        scratch_shapes=[pltpu.VMEM((bsz * DN_HEADS, DN_HEAD_DIM, DN_HEAD_DIM), F32)],
        compiler_params=_params("arbitrary"),
        name="delta_scan",
    )(u, w, qd, attn, kdt, egl)


def _mixer_out_kernel(h_ref, o_ref, yp_ref, wz_ref, wgd_ref, wgp_ref, wdn_ref, wpool_ref, wmix_ref,
                      dnw_ref, g_ref, b_ref, out_ref):
    h = h_ref[...]
    hb = h.astype(BF16)
    z = _dot(hb, wz_ref[...])
    o = o_ref[...]
    gated = []
    for hh in range(DN_HEADS):
        cs = slice(hh * DN_HEAD_DIM, (hh + 1) * DN_HEAD_DIM)
        oh = o[:, cs]
        oh = oh * lax.rsqrt(jnp.mean(oh * oh, axis=-1, keepdims=True) + RMS_EPS)
        gated.append((oh * dnw_ref[...] * _silu(z[:, cs])).astype(BF16))
    y_dn = _dot(jnp.concatenate(gated, axis=1), wdn_ref[...])
    y_pool = _dot(yp_ref[...], wpool_ref[...])
    merged = (jax.nn.sigmoid(_dot(hb, wgd_ref[...])) * y_dn
              + jax.nn.sigmoid(_dot(hb, wgp_ref[...])) * y_pool)
    mix = _dot(merged.astype(BF16), wmix_ref[...])
    out_ref[...] = _layernorm(ALPHA * h + mix, g_ref[...], b_ref[...])


def _mixer_out(h2d, o2d, yp2d, wz, wgd, wgp, wdn, wpool, wmix, dnw, g, b):
    t, d = h2d.shape
    tm = min(TOKEN_TILE, t)
    row = lambda w: pl.BlockSpec((tm, w), lambda i: (i, 0))
    consts = (wz, wgd, wgp, wdn, wpool, wmix, dnw, g, b)
    return pl.pallas_call(
        _mixer_out_kernel,
        grid=(t // tm,),
        in_specs=[row(d), row(DN_WIDTH), row(POOL_WIDTH)] + [_const_spec(c.shape) for c in consts],
        out_specs=row(d),
        out_shape=jax.ShapeDtypeStruct((t, d), F32),
        compiler_params=_params("parallel"),
        name="mixer_out",
    )(h2d, o2d, yp2d, *consts)


def _mem_kv_kernel(m_ref, g_ref, b_ref, wk_ref, wv_ref, k_ref, v_ref):
    m = _layernorm(m_ref[0], g_ref[...], b_ref[...]).astype(BF16)
    k_ref[0] = _dot(m, wk_ref[...]).astype(BF16)
    v_ref[0] = _dot(m, wv_ref[...]).astype(BF16)


def _mem_kv(mem, g, b, wk, wv):
    bsz, n_mem, d = mem.shape
    blk = pl.BlockSpec((1, n_mem, d), lambda i: (i, 0, 0))
    out = jax.ShapeDtypeStruct((bsz, n_mem, d), BF16)
    return pl.pallas_call(
        _mem_kv_kernel,
        grid=(bsz,),
        in_specs=[blk, _const_spec(g.shape), _const_spec(b.shape), _const_spec(wk.shape), _const_spec(wv.shape)],
        out_specs=[blk, blk],
        out_shape=[out, out],
        compiler_params=_params("parallel"),
        name="mem_kv",
    )(mem, g, b, wk, wv)


def _xattn_ln_kernel(x_ref, k_ref, v_ref, wq_ref, wo_ref, g_ref, b_ref, o_ref):
    x = x_ref[0]
    d = x.shape[-1]
    hd = d // XA_HEADS
    q = _dot(x.astype(BF16), wq_ref[...])
    heads = []
    for hh in range(XA_HEADS):
        cs = slice(hh * hd, (hh + 1) * hd)
        s = _dot_nt(q[:, cs].astype(BF16), k_ref[0, :, cs]) * (hd ** -0.5)
        e = jnp.exp(s - jnp.max(s, axis=-1, keepdims=True))
        pr = e * (1.0 / jnp.sum(e, axis=-1, keepdims=True))
        heads.append(_dot(pr.astype(BF16), v_ref[0, :, cs]).astype(BF16))
    xa = _dot(jnp.concatenate(heads, axis=1), wo_ref[...])
    o_ref[0] = _layernorm(ALPHA * x + xa, g_ref[...], b_ref[...])


def _xattn_ln(x, k, v, wq, wo, g, b):
    bsz, seq, d = x.shape
    n_mem = k.shape[1]
    tm = min(TOKEN_TILE, seq)
    row = pl.BlockSpec((1, tm, d), lambda bb, i: (bb, i, 0))
    kv = pl.BlockSpec((1, n_mem, d), lambda bb, i: (bb, 0, 0))
    return pl.pallas_call(
        _xattn_ln_kernel,
        grid=(bsz, seq // tm),
        in_specs=[row, kv, kv, _const_spec(wq.shape), _const_spec(wo.shape),
                  _const_spec(g.shape), _const_spec(b.shape)],
        out_specs=row,
        out_shape=jax.ShapeDtypeStruct((bsz, seq, d), F32),
        compiler_params=_params("parallel", "parallel"),
        name="xattn_ln",
    )(x, k, v, wq, wo, g, b)


def _row(vec):
    return vec.reshape(1, -1).astype(F32)


def _pad_lanes(vec):
    return jnp.zeros((1, AB_PAD), F32).at[0, :vec.shape[0]].set(vec.astype(F32))


def kernel(x, mem, ffn1_w_gate, ffn1_w_up, ffn1_w_down, ln1_g, ln1_b, w_in, conv_w, a_log, dt_bias, dn_norm_w, w_dn_branch, pool_w, pool_scale, w_pool_branch, w_mix_out, ln2_g, ln2_b, mem_ln_g, mem_ln_b, xa_wq, xa_wk, xa_wv, xa_wo, ln3_g, ln3_b, ffn2_w_gate, ffn2_w_up, ffn2_w_down, ln4_g, ln4_b):
    bsz, seq, d = x.shape
    t = bsz * seq
    bf = lambda w: w.astype(BF16)
    o_qkv = 3 * DN_WIDTH
    o_z = o_qkv + DN_WIDTH
    o_ab = o_z + 2 * DN_HEADS
    o_pool = o_ab + POOL_WIDTH
    o_gdn = o_pool + d

    for l in range(ffn1_w_gate.shape[0]):
        x2d = _ffn_ln(x.reshape(t, d), bf(ffn1_w_gate[l]), bf(ffn1_w_up[l]), bf(ffn1_w_down[l]),
                      _row(ln1_g[l]), _row(ln1_b[l]))

        wi = w_in[l]
        w_ab = jnp.zeros((d, AB_PAD), F32).at[:, :2 * DN_HEADS].set(wi[:, o_z:o_ab])
        wcat = bf(jnp.concatenate([wi[:, :o_qkv], wi[:, o_ab:o_pool], w_ab], axis=1))
        q, k, v, gb, ypool = _mixer_proj(x2d.reshape(bsz, seq, d), wcat, conv_w[l].astype(F32),
                                         _pad_lanes(a_log[l]), _pad_lanes(dt_bias[l]),
                                         bf(pool_w[l]), _row(pool_scale[l]))
        u, w, qd, attn, kdt, egl = _delta_prep(q, k, v, gb)
        o = _delta_scan(u, w, qd, attn, kdt, egl)
        x2d = _mixer_out(x2d, o.reshape(t, DN_WIDTH), ypool.reshape(t, POOL_WIDTH),
                         bf(wi[:, o_qkv:o_z]), bf(wi[:, o_pool:o_gdn]), bf(wi[:, o_gdn:]),
                         bf(w_dn_branch[l]), bf(w_pool_branch[l]), bf(w_mix_out[l]),
                         _row(dn_norm_w[l]), _row(ln2_g[l]), _row(ln2_b[l]))

        mk, mv = _mem_kv(mem, _row(mem_ln_g[l]), _row(mem_ln_b[l]), bf(xa_wk[l]), bf(xa_wv[l]))
        x3 = _xattn_ln(x2d.reshape(bsz, seq, d), mk, mv, bf(xa_wq[l]), bf(xa_wo[l]),
                       _row(ln3_g[l]), _row(ln3_b[l]))
        x = _ffn_ln(x3.reshape(t, d), bf(ffn2_w_gate[l]), bf(ffn2_w_up[l]), bf(ffn2_w_down[l]),
                    _row(ln4_g[l]), _row(ln4_b[l])).reshape(bsz, seq, d)
    return x
```

```python
import functools

import jax
import jax.numpy as jnp
from jax import lax
from jax.experimental import pallas as pl
from jax.experimental.pallas import tpu as pltpu

F32 = jnp.float32
BF16 = jnp.bfloat16

DN_HEADS = 4
DN_HEAD_DIM = 128
DN_WIDTH = DN_HEADS * DN_HEAD_DIM
CONV_K = 4
POOL_WINDOWS = (2, 4, 8, 16)
POOL_GROUP_DIM = 128
POOL_WIDTH = len(POOL_WINDOWS) * POOL_GROUP_DIM
XA_HEADS = 4
LN_EPS = 1e-5
RMS_EPS = 1e-6
L2_EPS = 1e-6
DEPTH = 1
ALPHA = (2.0 * DEPTH) ** 0.25

LANES = 128
HALO = 16
DELTA_CHUNK = 128
AB_PAD = LANES
VMEM_LIMIT = 56 * 1024 * 1024

TOKEN_TILE = 1024
PREP_CHUNKS = 4
MXU_DIM = 256
FF_SLICE = 3 * MXU_DIM


def _dot(a, b):
    return jnp.dot(a, b, preferred_element_type=F32)


def _dot_nt(a, b):
    return lax.dot_general(a, b, (((1,), (1,)), ((), ())), preferred_element_type=F32)


def _layernorm(y, g, b):
    mu = jnp.mean(y, axis=-1, keepdims=True)
    yc = y - mu
    var = jnp.mean(yc * yc, axis=-1, keepdims=True)
    return yc * lax.rsqrt(var + LN_EPS) * g + b


def _silu(x):
    return x * jax.nn.sigmoid(x)


def _const_spec(shape):
    nd = len(shape)
    return pl.BlockSpec(shape, lambda *_: (0,) * nd, pipeline_mode=pl.Buffered(1))


def _params(*sem):
    return pltpu.CompilerParams(dimension_semantics=sem, vmem_limit_bytes=VMEM_LIMIT)


def _ffn_ln_kernel(x_ref, wg_ref, wu_ref, wd_ref, g_ref, b_ref, o_ref):
    x = x_ref[...]
    xb = x.astype(BF16)
    ff = wg_ref.shape[1]
    y = None
    for lo in range(0, ff, FF_SLICE):
        cs = slice(lo, min(lo + FF_SLICE, ff))
        hg = _dot(xb, wg_ref[:, cs])
        hu = _dot(xb, wu_ref[:, cs])
        part = _dot((_silu(hg) * hu).astype(BF16), wd_ref[cs, :])
        y = part if y is None else y + part
    o_ref[...] = _layernorm(ALPHA * x + 0.5 * y, g_ref[...], b_ref[...])


def _ffn_ln(x2d, wg, wu, wd, g, b):
    t, d = x2d.shape
    ff = wg.shape[1]
    tm = min(TOKEN_TILE, t)
    row = pl.BlockSpec((tm, d), lambda i: (i, 0))
    return pl.pallas_call(
        _ffn_ln_kernel,
        grid=(t // tm,),
        in_specs=[row, _const_spec((d, ff)), _const_spec((d, ff)), _const_spec((ff, d)),
                  _const_spec((1, d)), _const_spec((1, d))],
        out_specs=row,
        out_shape=jax.ShapeDtypeStruct((t, d), F32),
        compiler_params=_params("parallel"),
        name="ffn_ln",
    )(x2d, wg, wu, wd, g, b)


def _mixer_proj_kernel(h_ref, halo_ref, wcat_ref, convw_ref, alog_ref, dtb_ref, poolw_ref,
                       pscale_ref, q_ref, k_ref, v_ref, gb_ref, yp_ref):
    i = pl.program_id(1)
    tm = h_ref.shape[1]
    qkv_w = 3 * DN_WIDTH
    halo = jnp.where(i == 0, 0.0, halo_ref[0])
    hcat = jnp.concatenate([halo, h_ref[0]], axis=0).astype(BF16)
    proj = _dot(hcat, wcat_ref[...])

    qkv_pre = proj[:, :qkv_w]
    conv = qkv_pre * convw_ref[CONV_K - 1:CONV_K, :]
    for s in range(1, CONV_K):
        conv = conv + pltpu.roll(qkv_pre, s, 0) * convw_ref[CONV_K - 1 - s:CONV_K - s, :]
    qkv = _silu(conv[HALO:, :])

    for part, ref, scale in ((0, q_ref, DN_HEAD_DIM ** -0.5), (1, k_ref, None), (2, v_ref, None)):
        for hh in range(DN_HEADS):
            lo = part * DN_WIDTH + hh * DN_HEAD_DIM
            seg = qkv[:, lo:lo + DN_HEAD_DIM]
            if part < 2:
                seg = seg * lax.rsqrt(jnp.sum(seg * seg, axis=-1, keepdims=True) + L2_EPS)
            if scale is not None:
                seg = seg * scale
            ref[0, :, hh * DN_HEAD_DIM:(hh + 1) * DN_HEAD_DIM] = seg

    ab = proj[HALO:, qkv_w + POOL_WIDTH:]
    xa = ab + dtb_ref[...]
    softplus = jnp.maximum(xa, 0.0) + jnp.log1p(jnp.exp(-jnp.abs(xa)))
    gval = -jnp.exp(alog_ref[...]) * softplus
    lane = lax.broadcasted_iota(jnp.int32, ab.shape, 1)
    gb_ref[0] = jnp.where(lane < DN_HEADS, gval, jax.nn.sigmoid(ab))

    t = (i * tm + lax.broadcasted_iota(jnp.int32, (tm, 1), 0) + 1).astype(F32)
    for gi, win in enumerate(POOL_WINDOWS):
        lo = qkv_w + gi * POOL_GROUP_DIM
        seg = proj[:, lo:lo + POOL_GROUP_DIM]
        wsum = seg
        sh = 1
        while sh < win:
            wsum = wsum + pltpu.roll(wsum, sh, 0)
            sh *= 2
        inv_cnt = 1.0 / jnp.minimum(t, float(win))
        mixed = wsum[HALO:, :] * inv_cnt - seg[HALO:, :]
        y = _dot(mixed.astype(BF16), poolw_ref[gi])
        cs = slice(gi * POOL_GROUP_DIM, (gi + 1) * POOL_GROUP_DIM)
        yp_ref[0, :, cs] = (y * pscale_ref[:, cs]).astype(BF16)


def _mixer_proj(h, wcat, conv_w, alog_pad, dtb_pad, pool_w, pool_scale):
    bsz, seq, d = h.shape
    tm = min(TOKEN_TILE, seq)
    hb = tm // HALO
    row = lambda w: pl.BlockSpec((1, tm, w), lambda b, i: (b, i, 0))
    out_f = jax.ShapeDtypeStruct((bsz, seq, DN_WIDTH), F32)
    return pl.pallas_call(
        _mixer_proj_kernel,
        grid=(bsz, seq // tm),
        in_specs=[row(d),
                  pl.BlockSpec((1, HALO, d), lambda b, i: (b, jnp.maximum(i * hb - 1, 0), 0)),
                  _const_spec(wcat.shape), _const_spec(conv_w.shape), _const_spec(alog_pad.shape),
                  _const_spec(dtb_pad.shape), _const_spec(pool_w.shape), _const_spec(pool_scale.shape)],
        out_specs=[row(DN_WIDTH), row(DN_WIDTH), row(DN_WIDTH), row(AB_PAD), row(POOL_WIDTH)],
        out_shape=[out_f, out_f, out_f,
                   jax.ShapeDtypeStruct((bsz, seq, AB_PAD), F32),
                   jax.ShapeDtypeStruct((bsz, seq, POOL_WIDTH), BF16)],
        compiler_params=_params("parallel", "parallel"),
        name="mixer_proj",
    )(h, h, wcat, conv_w, alog_pad, dtb_pad, pool_w, pool_scale)


def _delta_prep_kernel(q_ref, k_ref, v_ref, gb_ref, u_ref, w_ref, qd_ref, attn_ref, kdt_ref, egl_ref):
    c_len = DELTA_CHUNK
    n_chunks = q_ref.shape[1] // c_len
    ri = lax.broadcasted_iota(jnp.int32, (c_len, c_len), 0)
    ci = lax.broadcasted_iota(jnp.int32, (c_len, c_len), 1)
    incl = ri >= ci
    eye = ri == ci
    ltri = incl.astype(F32)
    level_masks = []
    s = 1
    while s < c_len:
        level_masks.append(((ri // (2 * s)) == (ci // (2 * s))) & ((ri // s) % 2 == 1) & ((ci // s) % 2 == 0))
        s *= 2
    lane = lax.broadcasted_iota(jnp.int32, (c_len, LANES), 1)

    chains = []
    for c in range(n_chunks):
        rows = slice(c * c_len, (c + 1) * c_len)
        gbc = gb_ref[0, rows, :]
        g_only = jnp.where(lane < DN_HEADS, gbc, 0.0)
        gc = jnp.dot(ltri, g_only, precision=lax.Precision.HIGHEST,
                     preferred_element_type=F32)
        gct = gc.T
        e_gc = jnp.exp(gc)
        egl_ref[0, c] = jnp.broadcast_to(jnp.exp(gct[0:8, c_len - 1:c_len]), (8, LANES))
        for hh in range(DN_HEADS):
            cs = slice(hh * DN_HEAD_DIM, (hh + 1) * DN_HEAD_DIM)
            chains.append(dict(
                rows=rows, cs=cs,
                beta=gbc[:, DN_HEADS + hh:DN_HEADS + hh + 1],
                gcol=gc[:, hh:hh + 1],
                grow=gct[hh:hh + 1, :],
                glast=gct[hh:hh + 1, c_len - 1:c_len],
                ecol=e_gc[:, hh:hh + 1]))

    for ch in chains:
        kh = k_ref[0, ch["rows"], ch["cs"]]
        ch["decay"] = jnp.exp(jnp.where(incl, ch["gcol"] - ch["grow"], -jnp.inf))
        ch["kb"] = kh * ch["beta"]
        ch["khb"] = kh.astype(BF16)
    for ch in chains:
        ch["a"] = jnp.where(eye, 0.0, _dot_nt(ch["kb"].astype(BF16), ch["khb"]) * ch["decay"])

    for ch in chains:
        ch["dinv"] = jnp.where(eye, 1.0, 0.0) - jnp.where(level_masks[0], ch["a"], 0.0)
    for m in level_masks[1:]:
        for ch in chains:
            ch["db"] = ch["dinv"].astype(BF16)
            ch["left"] = _dot(ch["db"], jnp.where(m, ch["a"], 0.0).astype(BF16))
        for ch in chains:
            ch["dinv"] = ch["dinv"] - _dot(ch["left"].astype(BF16), ch["db"])

    for ch in chains:
        rows, cs = ch["rows"], ch["cs"]
        t_low = jnp.where(eye, 0.0, ch["dinv"]).astype(BF16)
        rhs = jnp.concatenate([v_ref[0, rows, cs] * ch["beta"], ch["kb"] * ch["ecol"]], axis=1)
        sol = rhs + _dot(t_low, rhs.astype(BF16))
        u_ref[0, rows, cs] = sol[:, :DN_HEAD_DIM]
        w_ref[0, rows, cs] = sol[:, DN_HEAD_DIM:].astype(BF16)
    for ch in chains:
        rows, cs = ch["rows"], ch["cs"]
        qh = q_ref[0, rows, cs]
        qd_ref[0, rows, cs] = (qh * ch["ecol"]).astype(BF16)
        attn_ref[0, rows, cs] = (_dot_nt(qh.astype(BF16), ch["khb"]) * ch["decay"]).astype(BF16)
        kdt_ref[0, rows, cs] = (k_ref[0, rows, cs].T * jnp.exp(ch["glast"] - ch["grow"])).astype(BF16)


def _delta_prep(q, k, v, gb):
    bsz, seq, width = q.shape
    tm = min(PREP_CHUNKS * DELTA_CHUNK, seq)
    row = lambda w: pl.BlockSpec((1, tm, w), lambda b, i: (b, i, 0))
    n_chunks = seq // DELTA_CHUNK
    big = lambda dt: jax.ShapeDtypeStruct((bsz, seq, width), dt)
    return pl.pallas_call(
        _delta_prep_kernel,
        grid=(bsz, seq // tm),
        in_specs=[row(width), row(width), row(width), row(AB_PAD)],
        out_specs=[row(width)] * 5 + [pl.BlockSpec((1, tm // DELTA_CHUNK, 8, LANES), lambda b, i: (b, i, 0, 0))],
        out_shape=[big(F32), big(BF16), big(BF16), big(BF16), big(BF16),
                   jax.ShapeDtypeStruct((bsz, n_chunks, 8, LANES), F32)],
        compiler_params=_params("parallel", "parallel"),
        name="delta_prep",
    )(q, k, v, gb)


def _delta_scan_kernel(u_ref, w_ref, qd_ref, attn_ref, kdt_ref, egl_ref, o_ref, state_ref):
    @pl.when(pl.program_id(0) == 0)
    def _():
        state_ref[...] = jnp.zeros_like(state_ref)

    bsz = u_ref.shape[0]
    c_len = DELTA_CHUNK
    chains = [(b, hh, slice(hh * DN_HEAD_DIM, (hh + 1) * DN_HEAD_DIM))
              for b in range(bsz) for hh in range(DN_HEADS)]
    proj = [_dot(jnp.concatenate([w_ref[b, :, cs], qd_ref[b, :, cs]], axis=0),
                 state_ref[b * DN_HEADS + hh].astype(BF16))
            for b, hh, cs in chains]
    v_new = [(u_ref[b, :, cs] - p[:c_len]).astype(BF16) for (b, hh, cs), p in zip(chains, proj)]
    for (b, hh, cs), p, vn in zip(chains, proj, v_new):
        o_ref[b, :, cs] = p[c_len:] + _dot(attn_ref[b, :, cs], vn)
    for (b, hh, cs), vn in zip(chains, v_new):
        idx = b * DN_HEADS + hh
        state_ref[idx] = state_ref[idx] * egl_ref[b, 0, hh:hh + 1, :] + _dot(kdt_ref[b, :, cs], vn)


def _delta_scan(u, w, qd, attn, kdt, egl):
    bsz, seq, width = u.shape
    n_chunks = seq // DELTA_CHUNK
    blk = pl.BlockSpec((bsz, DELTA_CHUNK, width), lambda n: (0, n, 0))
    return pl.pallas_call(
        _delta_scan_kernel,
        grid=(n_chunks,),
        in_specs=[blk] * 5 + [pl.BlockSpec((bsz, 1, 8, LANES), lambda n: (0, n, 0, 0))],
        out_specs=blk,
        out_shape=jax.ShapeDtypeStruct((bsz, seq, width), F32),
        scratch_shapes=[pltpu.VMEM((bsz * DN_HEADS, DN_HEAD_DIM, DN_HEAD_DIM), F32)],
        compiler_params=_params("arbitrary"),
        name="delta_scan",
    )(u, w, qd, attn, kdt, egl)


def _mixer_out_kernel(h_ref, o_ref, yp_ref, wz_ref, wgd_ref, wgp_ref, wdn_ref, wpool_ref, wmix_ref,
                      dnw_ref, g_ref, b_ref, out_ref):
    h = h_ref[...]
    hb = h.astype(BF16)
    z = _dot(hb, wz_ref[...])
    o = o_ref[...]
    gated = []
    for hh in range(DN_HEADS):
        cs = slice(hh * DN_HEAD_DIM, (hh + 1) * DN_HEAD_DIM)
        oh = o[:, cs]
        oh = oh * lax.rsqrt(jnp.mean(oh * oh, axis=-1, keepdims=True) + RMS_EPS)
        gated.append((oh * dnw_ref[...] * _silu(z[:, cs])).astype(BF16))
    y_dn = _dot(jnp.concatenate(gated, axis=1), wdn_ref[...])
    y_pool = _dot(yp_ref[...], wpool_ref[...])
    merged = (jax.nn.sigmoid(_dot(hb, wgd_ref[...])) * y_dn
              + jax.nn.sigmoid(_dot(hb, wgp_ref[...])) * y_pool)
    mix = _dot(merged.astype(BF16), wmix_ref[...])
    out_ref[...] = _layernorm(ALPHA * h + mix, g_ref[...], b_ref[...])


def _mixer_out(h2d, o2d, yp2d, wz, wgd, wgp, wdn, wpool, wmix, dnw, g, b):
    t, d = h2d.shape
    tm = min(TOKEN_TILE, t)
    row = lambda w: pl.BlockSpec((tm, w), lambda i: (i, 0))
    consts = (wz, wgd, wgp, wdn, wpool, wmix, dnw, g, b)
    return pl.pallas_call(
        _mixer_out_kernel,
        grid=(t // tm,),
        in_specs=[row(d), row(DN_WIDTH), row(POOL_WIDTH)] + [_const_spec(c.shape) for c in consts],
        out_specs=row(d),
        out_shape=jax.ShapeDtypeStruct((t, d), F32),
        compiler_params=_params("parallel"),
        name="mixer_out",
    )(h2d, o2d, yp2d, *consts)


def _mem_kv_kernel(m_ref, g_ref, b_ref, wk_ref, wv_ref, k_ref, v_ref):
    m = _layernorm(m_ref[0], g_ref[...], b_ref[...]).astype(BF16)
    k_ref[0] = _dot(m, wk_ref[...]).astype(BF16)
    v_ref[0] = _dot(m, wv_ref[...]).astype(BF16)


def _mem_kv(mem, g, b, wk, wv):
    bsz, n_mem, d = mem.shape
    blk = pl.BlockSpec((1, n_mem, d), lambda i: (i, 0, 0))
    out = jax.ShapeDtypeStruct((bsz, n_mem, d), BF16)
    return pl.pallas_call(
        _mem_kv_kernel,
        grid=(bsz,),
        in_specs=[blk, _const_spec(g.shape), _const_spec(b.shape), _const_spec(wk.shape), _const_spec(wv.shape)],
        out_specs=[blk, blk],
        out_shape=[out, out],
        compiler_params=_params("parallel"),
        name="mem_kv",
    )(mem, g, b, wk, wv)


def _xattn_ln_kernel(x_ref, k_ref, v_ref, wq_ref, wo_ref, g_ref, b_ref, o_ref):
    x = x_ref[0]
    d = x.shape[-1]
    hd = d // XA_HEADS
    q = _dot(x.astype(BF16), wq_ref[...])
    heads = []
    for hh in range(XA_HEADS):
        cs = slice(hh * hd, (hh + 1) * hd)
        s = _dot_nt(q[:, cs].astype(BF16), k_ref[0, :, cs]) * (hd ** -0.5)
        e = jnp.exp(s - jnp.max(s, axis=-1, keepdims=True))
        pr = e * (1.0 / jnp.sum(e, axis=-1, keepdims=True))
        heads.append(_dot(pr.astype(BF16), v_ref[0, :, cs]).astype(BF16))
    xa = _dot(jnp.concatenate(heads, axis=1), wo_ref[...])
    o_ref[0] = _layernorm(ALPHA * x + xa, g_ref[...], b_ref[...])


def _xattn_ln(x, k, v, wq, wo, g, b):
    bsz, seq, d = x.shape
    n_mem = k.shape[1]
    tm = min(TOKEN_TILE, seq)
    row = pl.BlockSpec((1, tm, d), lambda bb, i: (bb, i, 0))
    kv = pl.BlockSpec((1, n_mem, d), lambda bb, i: (bb, 0, 0))
    return pl.pallas_call(
        _xattn_ln_kernel,
        grid=(bsz, seq // tm),
        in_specs=[row, kv, kv, _const_spec(wq.shape), _const_spec(wo.shape),
                  _const_spec(g.shape), _const_spec(b.shape)],
        out_specs=row,
        out_shape=jax.ShapeDtypeStruct((bsz, seq, d), F32),
        compiler_params=_params("parallel", "parallel"),
        name="xattn_ln",
    )(x, k, v, wq, wo, g, b)


def _row(vec):
    return vec.reshape(1, -1).astype(F32)


def _pad_lanes(vec):
    return jnp.zeros((1, AB_PAD), F32).at[0, :vec.shape[0]].set(vec.astype(F32))


def kernel(x, mem, ffn1_w_gate, ffn1_w_up, ffn1_w_down, ln1_g, ln1_b, w_in, conv_w, a_log, dt_bias, dn_norm_w, w_dn_branch, pool_w, pool_scale, w_pool_branch, w_mix_out, ln2_g, ln2_b, mem_ln_g, mem_ln_b, xa_wq, xa_wk, xa_wv, xa_wo, ln3_g, ln3_b, ffn2_w_gate, ffn2_w_up, ffn2_w_down, ln4_g, ln4_b):
    bsz, seq, d = x.shape
    t = bsz * seq
    bf = lambda w: w.astype(BF16)
    o_qkv = 3 * DN_WIDTH
    o_z = o_qkv + DN_WIDTH
    o_ab = o_z + 2 * DN_HEADS
    o_pool = o_ab + POOL_WIDTH
    o_gdn = o_pool + d

    for l in range(ffn1_w_gate.shape[0]):
        x2d = _ffn_ln(x.reshape(t, d), bf(ffn1_w_gate[l]), bf(ffn1_w_up[l]), bf(ffn1_w_down[l]),
                      _row(ln1_g[l]), _row(ln1_b[l]))

        wi = w_in[l]
        w_ab = jnp.zeros((d, AB_PAD), F32).at[:, :2 * DN_HEADS].set(wi[:, o_z:o_ab])
        wcat = bf(jnp.concatenate([wi[:, :o_qkv], wi[:, o_ab:o_pool], w_ab], axis=1))
        q, k, v, gb, ypool = _mixer_proj(x2d.reshape(bsz, seq, d), wcat, conv_w[l].astype(F32),
                                         _pad_lanes(a_log[l]), _pad_lanes(dt_bias[l]),
                                         bf(pool_w[l]), _row(pool_scale[l]))
        u, w, qd, attn, kdt, egl = _delta_prep(q, k, v, gb)
        o = _delta_scan(u, w, qd, attn, kdt, egl)
        x2d = _mixer_out(x2d, o.reshape(t, DN_WIDTH), ypool.reshape(t, POOL_WIDTH),
                         bf(wi[:, o_qkv:o_z]), bf(wi[:, o_pool:o_gdn]), bf(wi[:, o_gdn:]),
                         bf(w_dn_branch[l]), bf(w_pool_branch[l]), bf(w_mix_out[l]),
                         _row(dn_norm_w[l]), _row(ln2_g[l]), _row(ln2_b[l]))

        mk, mv = _mem_kv(mem, _row(mem_ln_g[l]), _row(mem_ln_b[l]), bf(xa_wk[l]), bf(xa_wv[l]))
        x3 = _xattn_ln(x2d.reshape(bsz, seq, d), mk, mv, bf(xa_wq[l]), bf(xa_wo[l]),
                       _row(ln3_g[l]), _row(ln3_b[l]))
        x = _ffn_ln(x3.reshape(t, d), bf(ffn2_w_gate[l]), bf(ffn2_w_up[l]), bf(ffn2_w_down[l]),
                    _row(ln4_g[l]), _row(ln4_b[l])).reshape(bsz, seq, d)
    return x
```

```python
import functools

import jax
import jax.numpy as jnp
from jax import lax
from jax.experimental import pallas as pl
from jax.experimental.pallas import tpu as pltpu

F32 = jnp.float32
BF16 = jnp.bfloat16

DN_HEADS = 4
DN_HEAD_DIM = 128
DN_WIDTH = DN_HEADS * DN_HEAD_DIM
CONV_K = 4
POOL_WINDOWS = (2, 4, 8, 16)
POOL_GROUP_DIM = 128
POOL_WIDTH = len(POOL_WINDOWS) * POOL_GROUP_DIM
XA_HEADS = 4
LN_EPS = 1e-5
RMS_EPS = 1e-6
L2_EPS = 1e-6
DEPTH = 1
ALPHA = (2.0 * DEPTH) ** 0.25

LANES = 128
HALO = 16
DELTA_CHUNK = 128
AB_PAD = LANES
VMEM_LIMIT = 56 * 1024 * 1024

TOKEN_TILE = 1024
MXU_DIM = 256
FF_SLICE = 3 * MXU_DIM


def _dot(a, b):
    return jnp.dot(a, b, preferred_element_type=F32)


def _dot_nt(a, b):
    return lax.dot_general(a, b, (((1,), (1,)), ((), ())), preferred_element_type=F32)


def _layernorm(y, g, b):
    mu = jnp.mean(y, axis=-1, keepdims=True)
    yc = y - mu
    var = jnp.mean(yc * yc, axis=-1, keepdims=True)
    return yc * lax.rsqrt(var + LN_EPS) * g + b


def _silu(x):
    return x * jax.nn.sigmoid(x)


def _const_spec(shape):
    nd = len(shape)
    return pl.BlockSpec(shape, lambda *_: (0,) * nd, pipeline_mode=pl.Buffered(1))


def _params(*sem):
    return pltpu.CompilerParams(dimension_semantics=sem, vmem_limit_bytes=VMEM_LIMIT)


def _ffn_ln_kernel(x_ref, wg_ref, wu_ref, wd_ref, g_ref, b_ref, o_ref):
    x = x_ref[...]
    xb = x.astype(BF16)
    ff = wg_ref.shape[1]
    y = None
    for lo in range(0, ff, FF_SLICE):
        cs = slice(lo, min(lo + FF_SLICE, ff))
        hg = _dot(xb, wg_ref[:, cs])
        hu = _dot(xb, wu_ref[:, cs])
        part = _dot((_silu(hg) * hu).astype(BF16), wd_ref[cs, :])
        y = part if y is None else y + part
    o_ref[...] = _layernorm(ALPHA * x + 0.5 * y, g_ref[...], b_ref[...])


def _ffn_ln(x2d, wg, wu, wd, g, b):
    t, d = x2d.shape
    ff = wg.shape[1]
    tm = min(TOKEN_TILE, t)
    row = pl.BlockSpec((tm, d), lambda i: (i, 0))
    return pl.pallas_call(
        _ffn_ln_kernel,
        grid=(t // tm,),
        in_specs=[row, _const_spec((d, ff)), _const_spec((d, ff)), _const_spec((ff, d)),
                  _const_spec((1, d)), _const_spec((1, d))],
        out_specs=row,
        out_shape=jax.ShapeDtypeStruct((t, d), F32),
        compiler_params=_params("parallel"),
        name="ffn_ln",
    )(x2d, wg, wu, wd, g, b)


def _mixer_proj_kernel(h_ref, halo_ref, wcat_ref, convw_ref, alog_ref, dtb_ref, poolw_ref,
                       pscale_ref, q_ref, k_ref, v_ref, gb_ref, yp_ref):
    i = pl.program_id(1)
    tm = h_ref.shape[1]
    qkv_w = 3 * DN_WIDTH
    halo = jnp.where(i == 0, 0.0, halo_ref[0])
    hcat = jnp.concatenate([halo, h_ref[0]], axis=0).astype(BF16)
    proj = _dot(hcat, wcat_ref[...])

    qkv_pre = proj[:, :qkv_w]
    conv = qkv_pre * convw_ref[CONV_K - 1:CONV_K, :]
    for s in range(1, CONV_K):
        conv = conv + pltpu.roll(qkv_pre, s, 0) * convw_ref[CONV_K - 1 - s:CONV_K - s, :]
    qkv = _silu(conv[HALO:, :])

    for part, ref, scale in ((0, q_ref, DN_HEAD_DIM ** -0.5), (1, k_ref, None), (2, v_ref, None)):
        for hh in range(DN_HEADS):
            lo = part * DN_WIDTH + hh * DN_HEAD_DIM
            seg = qkv[:, lo:lo + DN_HEAD_DIM]
            if part < 2:
                seg = seg * lax.rsqrt(jnp.sum(seg * seg, axis=-1, keepdims=True) + L2_EPS)
            if scale is not None:
                seg = seg * scale
            ref[0, :, hh * DN_HEAD_DIM:(hh + 1) * DN_HEAD_DIM] = seg

    ab = proj[HALO:, qkv_w + POOL_WIDTH:]
    xa = ab + dtb_ref[...]
    softplus = jnp.maximum(xa, 0.0) + jnp.log1p(jnp.exp(-jnp.abs(xa)))
    gval = -jnp.exp(alog_ref[...]) * softplus
    lane = lax.broadcasted_iota(jnp.int32, ab.shape, 1)
    gb_ref[0] = jnp.where(lane < DN_HEADS, gval, jax.nn.sigmoid(ab))

    t = (i * tm + lax.broadcasted_iota(jnp.int32, (tm, 1), 0) + 1).astype(F32)
    for gi, win in enumerate(POOL_WINDOWS):
        lo = qkv_w + gi * POOL_GROUP_DIM
        seg = proj[:, lo:lo + POOL_GROUP_DIM]
        wsum = seg
        sh = 1
        while sh < win:
            wsum = wsum + pltpu.roll(wsum, sh, 0)
            sh *= 2
        inv_cnt = 1.0 / jnp.minimum(t, float(win))
        mixed = wsum[HALO:, :] * inv_cnt - seg[HALO:, :]
        y = _dot(mixed.astype(BF16), poolw_ref[gi])
        cs = slice(gi * POOL_GROUP_DIM, (gi + 1) * POOL_GROUP_DIM)
        yp_ref[0, :, cs] = (y * pscale_ref[:, cs]).astype(BF16)


def _mixer_proj(h, wcat, conv_w, alog_pad, dtb_pad, pool_w, pool_scale):
    bsz, seq, d = h.shape
    tm = min(TOKEN_TILE, seq)
    hb = tm // HALO
    row = lambda w: pl.BlockSpec((1, tm, w), lambda b, i: (b, i, 0))
    out_f = jax.ShapeDtypeStruct((bsz, seq, DN_WIDTH), F32)
    return pl.pallas_call(
        _mixer_proj_kernel,
        grid=(bsz, seq // tm),
        in_specs=[row(d),
                  pl.BlockSpec((1, HALO, d), lambda b, i: (b, jnp.maximum(i * hb - 1, 0), 0)),
                  _const_spec(wcat.shape), _const_spec(conv_w.shape), _const_spec(alog_pad.shape),
                  _const_spec(dtb_pad.shape), _const_spec(pool_w.shape), _const_spec(pool_scale.shape)],
        out_specs=[row(DN_WIDTH), row(DN_WIDTH), row(DN_WIDTH), row(AB_PAD), row(POOL_WIDTH)],
        out_shape=[out_f, out_f, out_f,
                   jax.ShapeDtypeStruct((bsz, seq, AB_PAD), F32),
                   jax.ShapeDtypeStruct((bsz, seq, POOL_WIDTH), BF16)],
        compiler_params=_params("parallel", "parallel"),
        name="mixer_proj",
    )(h, h, wcat, conv_w, alog_pad, dtb_pad, pool_w, pool_scale)


def _delta_kernel(q_ref, k_ref, v_ref, gb_ref, o_ref, state_ref):
    @pl.when(pl.program_id(0) == 0)
    def _():
        state_ref[...] = jnp.zeros_like(state_ref)

    c_len = DELTA_CHUNK
    bsz = q_ref.shape[0]
    ri = lax.broadcasted_iota(jnp.int32, (c_len, c_len), 0)
    ci = lax.broadcasted_iota(jnp.int32, (c_len, c_len), 1)
    incl = ri >= ci
    eye = ri == ci
    ltri = incl.astype(F32)
    level_masks = []
    s = 1
    while s < c_len:
        level_masks.append(((ri // (2 * s)) == (ci // (2 * s))) & ((ri // s) % 2 == 1) & ((ci // s) % 2 == 0))
        s *= 2
    lane = lax.broadcasted_iota(jnp.int32, (c_len, LANES), 1)

    chains = []
    for b in range(bsz):
        gbc = gb_ref[b]
        g_only = jnp.where(lane < DN_HEADS, gbc, 0.0)
        gc = jnp.dot(ltri, g_only, precision=lax.Precision.HIGHEST,
                     preferred_element_type=F32)
        gct = gc.T
        e_gc = jnp.exp(gc)
        for hh in range(DN_HEADS):
            cs = slice(hh * DN_HEAD_DIM, (hh + 1) * DN_HEAD_DIM)
            chains.append(dict(
                b=b, idx=b * DN_HEADS + hh, cs=cs,
                beta=gbc[:, DN_HEADS + hh:DN_HEADS + hh + 1],
                gcol=gc[:, hh:hh + 1],
                grow=gct[hh:hh + 1, :],
                glast=gct[hh:hh + 1, c_len - 1:c_len],
                ecol=e_gc[:, hh:hh + 1]))

    for ch in chains:
        kh = k_ref[ch["b"], :, ch["cs"]]
        ch["decay"] = jnp.exp(jnp.where(incl, ch["gcol"] - ch["grow"], -jnp.inf))
        ch["kb"] = kh * ch["beta"]
        ch["khb"] = kh.astype(BF16)
    for ch in chains:
        ch["a"] = jnp.where(eye, 0.0, _dot_nt(ch["kb"].astype(BF16), ch["khb"]) * ch["decay"])

    for ch in chains:
        ch["dinv"] = jnp.where(eye, 1.0, 0.0) - jnp.where(level_masks[0], ch["a"], 0.0)
    for m in level_masks[1:]:
        for ch in chains:
            ch["db"] = ch["dinv"].astype(BF16)
            ch["left"] = _dot(ch["db"], jnp.where(m, ch["a"], 0.0).astype(BF16))
        for ch in chains:
            ch["dinv"] = ch["dinv"] - _dot(ch["left"].astype(BF16), ch["db"])

    for ch in chains:
        b, cs = ch["b"], ch["cs"]
        t_low = jnp.where(eye, 0.0, ch["dinv"]).astype(BF16)
        rhs = jnp.concatenate([v_ref[b, :, cs] * ch["beta"], ch["kb"] * ch["ecol"]], axis=1)
        sol = rhs + _dot(t_low, rhs.astype(BF16))
        ch["u"] = sol[:, :DN_HEAD_DIM]
        qh = q_ref[b, :, cs]
        ch["wq"] = jnp.concatenate([sol[:, DN_HEAD_DIM:].astype(BF16), (qh * ch["ecol"]).astype(BF16)], axis=0)
        ch["attn"] = (_dot_nt(qh.astype(BF16), ch["khb"]) * ch["decay"]).astype(BF16)
        ch["kdt"] = (k_ref[b, :, cs].T * jnp.exp(ch["glast"] - ch["grow"])).astype(BF16)

    for ch in chains:
        ch["proj"] = _dot(ch["wq"], state_ref[ch["idx"]].astype(BF16))
    for ch in chains:
        ch["v_new"] = (ch["u"] - ch["proj"][:c_len]).astype(BF16)
    for ch in chains:
        o_ref[ch["b"], :, ch["cs"]] = ch["proj"][c_len:] + _dot(ch["attn"], ch["v_new"])
    for ch in chains:
        idx = ch["idx"]
        state_ref[idx] = state_ref[idx] * jnp.exp(ch["glast"]) + _dot(ch["kdt"], ch["v_new"])


def _delta_rule(q, k, v, gb):
    bsz, seq, width = q.shape
    blk = lambda w: pl.BlockSpec((bsz, DELTA_CHUNK, w), lambda n: (0, n, 0))
    return pl.pallas_call(
        _delta_kernel,
        grid=(seq // DELTA_CHUNK,),
        in_specs=[blk(width), blk(width), blk(width), blk(AB_PAD)],
        out_specs=blk(width),
        out_shape=jax.ShapeDtypeStruct((bsz, seq, width), F32),
        scratch_shapes=[pltpu.VMEM((bsz * DN_HEADS, DN_HEAD_DIM, DN_HEAD_DIM), F32)],
        compiler_params=_params("arbitrary"),
        name="delta_rule",
    )(q, k, v, gb)


def _mixer_out_kernel(h_ref, o_ref, yp_ref, wz_ref, wgd_ref, wgp_ref, wdn_ref, wpool_ref, wmix_ref,
                      dnw_ref, g_ref, b_ref, out_ref):
    h = h_ref[...]
    hb = h.astype(BF16)
    z = _dot(hb, wz_ref[...])
    o = o_ref[...]
    gated = []
    for hh in range(DN_HEADS):
        cs = slice(hh * DN_HEAD_DIM, (hh + 1) * DN_HEAD_DIM)
        oh = o[:, cs]
        oh = oh * lax.rsqrt(jnp.mean(oh * oh, axis=-1, keepdims=True) + RMS_EPS)
        gated.append((oh * dnw_ref[...] * _silu(z[:, cs])).astype(BF16))
    y_dn = _dot(jnp.concatenate(gated, axis=1), wdn_ref[...])
    y_pool = _dot(yp_ref[...], wpool_ref[...])
    merged = (jax.nn.sigmoid(_dot(hb, wgd_ref[...])) * y_dn
              + jax.nn.sigmoid(_dot(hb, wgp_ref[...])) * y_pool)
    mix = _dot(merged.astype(BF16), wmix_ref[...])
    out_ref[...] = _layernorm(ALPHA * h + mix, g_ref[...], b_ref[...])


def _mixer_out(h2d, o2d, yp2d, wz, wgd, wgp, wdn, wpool, wmix, dnw, g, b):
    t, d = h2d.shape
    tm = min(TOKEN_TILE, t)
    row = lambda w: pl.BlockSpec((tm, w), lambda i: (i, 0))
    consts = (wz, wgd, wgp, wdn, wpool, wmix, dnw, g, b)
    return pl.pallas_call(
        _mixer_out_kernel,
        grid=(t // tm,),
        in_specs=[row(d), row(DN_WIDTH), row(POOL_WIDTH)] + [_const_spec(c.shape) for c in consts],
        out_specs=row(d),
        out_shape=jax.ShapeDtypeStruct((t, d), F32),
        compiler_params=_params("parallel"),
        name="mixer_out",
    )(h2d, o2d, yp2d, *consts)


def _mem_kv_kernel(m_ref, g_ref, b_ref, wk_ref, wv_ref, k_ref, v_ref):
    m = _layernorm(m_ref[0], g_ref[...], b_ref[...]).astype(BF16)
    k_ref[0] = _dot(m, wk_ref[...]).astype(BF16)
    v_ref[0] = _dot(m, wv_ref[...]).astype(BF16)


def _mem_kv(mem, g, b, wk, wv):
    bsz, n_mem, d = mem.shape
    blk = pl.BlockSpec((1, n_mem, d), lambda i: (i, 0, 0))
    out = jax.ShapeDtypeStruct((bsz, n_mem, d), BF16)
    return pl.pallas_call(
        _mem_kv_kernel,
        grid=(bsz,),
        in_specs=[blk, _const_spec(g.shape), _const_spec(b.shape), _const_spec(wk.shape), _const_spec(wv.shape)],
        out_specs=[blk, blk],
        out_shape=[out, out],
        compiler_params=_params("parallel"),
        name="mem_kv",
    )(mem, g, b, wk, wv)


def _xattn_ln_kernel(x_ref, k_ref, v_ref, wq_ref, wo_ref, g_ref, b_ref, o_ref):
    x = x_ref[0]
    d = x.shape[-1]
    hd = d // XA_HEADS
    q = _dot(x.astype(BF16), wq_ref[...])
    heads = []
    for hh in range(XA_HEADS):
        cs = slice(hh * hd, (hh + 1) * hd)
        s = _dot_nt(q[:, cs].astype(BF16), k_ref[0, :, cs]) * (hd ** -0.5)
        e = jnp.exp(s - jnp.max(s, axis=-1, keepdims=True))
        pr = e * (1.0 / jnp.sum(e, axis=-1, keepdims=True))
        heads.append(_dot(pr.astype(BF16), v_ref[0, :, cs]).astype(BF16))
    xa = _dot(jnp.concatenate(heads, axis=1), wo_ref[...])
    o_ref[0] = _layernorm(ALPHA * x + xa, g_ref[...], b_ref[...])


def _xattn_ln(x, k, v, wq, wo, g, b):
    bsz, seq, d = x.shape
    n_mem = k.shape[1]
    tm = min(TOKEN_TILE, seq)
    row = pl.BlockSpec((1, tm, d), lambda bb, i: (bb, i, 0))
    kv = pl.BlockSpec((1, n_mem, d), lambda bb, i: (bb, 0, 0))
    return pl.pallas_call(
        _xattn_ln_kernel,
        grid=(bsz, seq // tm),
        in_specs=[row, kv, kv, _const_spec(wq.shape), _const_spec(wo.shape),
                  _const_spec(g.shape), _const_spec(b.shape)],
        out_specs=row,
        out_shape=jax.ShapeDtypeStruct((bsz, seq, d), F32),
        compiler_params=_params("parallel", "parallel"),
        name="xattn_ln",
    )(x, k, v, wq, wo, g, b)


def _row(vec):
    return vec.reshape(1, -1).astype(F32)


def _pad_lanes(vec):
    return jnp.zeros((1, AB_PAD), F32).at[0, :vec.shape[0]].set(vec.astype(F32))


def kernel(x, mem, ffn1_w_gate, ffn1_w_up, ffn1_w_down, ln1_g, ln1_b, w_in, conv_w, a_log, dt_bias, dn_norm_w, w_dn_branch, pool_w, pool_scale, w_pool_branch, w_mix_out, ln2_g, ln2_b, mem_ln_g, mem_ln_b, xa_wq, xa_wk, xa_wv, xa_wo, ln3_g, ln3_b, ffn2_w_gate, ffn2_w_up, ffn2_w_down, ln4_g, ln4_b):
    bsz, seq, d = x.shape
    t = bsz * seq
    bf = lambda w: w.astype(BF16)
    o_qkv = 3 * DN_WIDTH
    o_z = o_qkv + DN_WIDTH
    o_ab = o_z + 2 * DN_HEADS
    o_pool = o_ab + POOL_WIDTH
    o_gdn = o_pool + d

    for l in range(ffn1_w_gate.shape[0]):
        x2d = _ffn_ln(x.reshape(t, d), bf(ffn1_w_gate[l]), bf(ffn1_w_up[l]), bf(ffn1_w_down[l]),
                      _row(ln1_g[l]), _row(ln1_b[l]))

        wi = w_in[l]
        w_ab = jnp.zeros((d, AB_PAD), F32).at[:, :2 * DN_HEADS].set(wi[:, o_z:o_ab])
        wcat = bf(jnp.concatenate([wi[:, :o_qkv], wi[:, o_ab:o_pool], w_ab], axis=1))
        q, k, v, gb, ypool = _mixer_proj(x2d.reshape(bsz, seq, d), wcat, conv_w[l].astype(F32),
                                         _pad_lanes(a_log[l]), _pad_lanes(dt_bias[l]),
                                         bf(pool_w[l]), _row(pool_scale[l]))
        o = _delta_rule(q, k, v, gb)
        x2d = _mixer_out(x2d, o.reshape(t, DN_WIDTH), ypool.reshape(t, POOL_WIDTH),
                         bf(wi[:, o_qkv:o_z]), bf(wi[:, o_pool:o_gdn]), bf(wi[:, o_gdn:]),
                         bf(w_dn_branch[l]), bf(w_pool_branch[l]), bf(w_mix_out[l]),
                         _row(dn_norm_w[l]), _row(ln2_g[l]), _row(ln2_b[l]))

        mk, mv = _mem_kv(mem, _row(mem_ln_g[l]), _row(mem_ln_b[l]), bf(xa_wk[l]), bf(xa_wv[l]))
        x3 = _xattn_ln(x2d.reshape(bsz, seq, d), mk, mv, bf(xa_wq[l]), bf(xa_wo[l]),
                       _row(ln3_g[l]), _row(ln3_b[l]))
        x = _ffn_ln(x3.reshape(t, d), bf(ffn2_w_gate[l]), bf(ffn2_w_up[l]), bf(ffn2_w_down[l]),
                    _row(ln4_g[l]), _row(ln4_b[l])).reshape(bsz, seq, d)
    return x
```

```python
import functools

import jax
import jax.numpy as jnp
from jax import lax
from jax.experimental import pallas as pl
from jax.experimental.pallas import tpu as pltpu

F32 = jnp.float32
BF16 = jnp.bfloat16

DN_HEADS = 4
DN_HEAD_DIM = 128
DN_WIDTH = DN_HEADS * DN_HEAD_DIM
CONV_K = 4
POOL_WINDOWS = (2, 4, 8, 16)
POOL_GROUP_DIM = 128
POOL_WIDTH = len(POOL_WINDOWS) * POOL_GROUP_DIM
XA_HEADS = 4
LN_EPS = 1e-5
RMS_EPS = 1e-6
L2_EPS = 1e-6
DEPTH = 1
ALPHA = (2.0 * DEPTH) ** 0.25

LANES = 128
BF16_ROWS = 16
HALO = 16
DELTA_CHUNK = 128
AB_PAD = LANES
VMEM_LIMIT = 56 * 1024 * 1024

TOKEN_TILE = 1024
DELTA_STEP_CHUNKS = 2
MXU_DIM = 256
FF_SLICE = 3 * MXU_DIM


def _dot(a, b):
    return jnp.dot(a, b, preferred_element_type=F32)


def _dot_nt(a, b):
    return lax.dot_general(a, b, (((1,), (1,)), ((), ())), preferred_element_type=F32)


def _layernorm(y, g, b):
    mu = jnp.mean(y, axis=-1, keepdims=True)
    yc = y - mu
    var = jnp.mean(yc * yc, axis=-1, keepdims=True)
    return yc * lax.rsqrt(var + LN_EPS) * g + b


def _silu(x):
    return x * jax.nn.sigmoid(x)


def _const_spec(shape):
    nd = len(shape)
    return pl.BlockSpec(shape, lambda *_: (0,) * nd, pipeline_mode=pl.Buffered(1))


def _params(*sem):
    return pltpu.CompilerParams(dimension_semantics=sem, vmem_limit_bytes=VMEM_LIMIT)


def _ffn_ln_kernel(n_riders, x_ref, wg_ref, wu_ref, wd_ref, g_ref, b_ref, *refs):
    rider_in, o_ref, rider_out = refs[:n_riders], refs[n_riders], refs[n_riders + 1:]
    for src, dst in zip(rider_in, rider_out):
        dst[...] = src[...].astype(BF16)

    x = x_ref[...]
    xb = x.astype(BF16)
    ff = wg_ref.shape[1]
    y = None
    for lo in range(0, ff, FF_SLICE):
        cs = slice(lo, min(lo + FF_SLICE, ff))
        hg = _dot(xb, wg_ref[:, cs])
        hu = _dot(xb, wu_ref[:, cs])
        part = _dot((_silu(hg) * hu).astype(BF16), wd_ref[cs, :])
        y = part if y is None else y + part
    o_ref[...] = _layernorm(ALPHA * x + 0.5 * y, g_ref[...], b_ref[...])


def _ffn_ln(x2d, wg, wu, wd, g, b, riders=()):
    t, d = x2d.shape
    ff = wg.shape[1]
    tm = min(TOKEN_TILE, t)
    steps = t // tm
    row = pl.BlockSpec((tm, d), lambda i: (i, 0))
    rider_specs = []
    for w in riders:
        rows = w.shape[0] // steps
        assert w.shape[0] == rows * steps and rows % BF16_ROWS == 0, w.shape
        rider_specs.append(pl.BlockSpec((rows, w.shape[1]), lambda i: (i, 0)))
    outs = pl.pallas_call(
        functools.partial(_ffn_ln_kernel, len(riders)),
        grid=(steps,),
        in_specs=[row, _const_spec((d, ff)), _const_spec((d, ff)), _const_spec((ff, d)),
                  _const_spec((1, d)), _const_spec((1, d))] + rider_specs,
        out_specs=[row] + rider_specs,
        out_shape=[jax.ShapeDtypeStruct((t, d), F32)] + [jax.ShapeDtypeStruct(w.shape, BF16) for w in riders],
        compiler_params=_params("parallel"),
        name="ffn_ln",
    )(x2d, wg, wu, wd, g, b, *riders)
    return outs[0], outs[1:]


def _mixer_proj_kernel(h_ref, halo_ref, wcat_ref, convw_ref, alog_ref, dtb_ref, poolw_ref,
                       pscale_ref, q_ref, k_ref, v_ref, gb_ref, yp_ref):
    i = pl.program_id(1)
    tm = h_ref.shape[1]
    qkv_w = 3 * DN_WIDTH
    halo = jnp.where(i == 0, 0.0, halo_ref[0])
    hcat = jnp.concatenate([halo, h_ref[0]], axis=0).astype(BF16)
    proj = _dot(hcat, wcat_ref[...])

    qkv_pre = proj[:, :qkv_w]
    conv = qkv_pre * convw_ref[CONV_K - 1:CONV_K, :]
    for s in range(1, CONV_K):
        conv = conv + pltpu.roll(qkv_pre, s, 0) * convw_ref[CONV_K - 1 - s:CONV_K - s, :]
    qkv = _silu(conv[HALO:, :])

    for part, ref, scale in ((0, q_ref, DN_HEAD_DIM ** -0.5), (1, k_ref, None), (2, v_ref, None)):
        for hh in range(DN_HEADS):
            lo = part * DN_WIDTH + hh * DN_HEAD_DIM
            seg = qkv[:, lo:lo + DN_HEAD_DIM]
            if part < 2:
                seg = seg * lax.rsqrt(jnp.sum(seg * seg, axis=-1, keepdims=True) + L2_EPS)
            if scale is not None:
                seg = seg * scale
            ref[0, :, hh * DN_HEAD_DIM:(hh + 1) * DN_HEAD_DIM] = seg

    ab = proj[HALO:, qkv_w + POOL_WIDTH:]
    xa = ab + dtb_ref[...]
    softplus = jnp.maximum(xa, 0.0) + jnp.log1p(jnp.exp(-jnp.abs(xa)))
    gval = -jnp.exp(alog_ref[...]) * softplus
    lane = lax.broadcasted_iota(jnp.int32, ab.shape, 1)
    gb_ref[0] = jnp.where(lane < DN_HEADS, gval, jax.nn.sigmoid(ab))

    t = (i * tm + lax.broadcasted_iota(jnp.int32, (tm, 1), 0) + 1).astype(F32)
    for gi, win in enumerate(POOL_WINDOWS):
        lo = qkv_w + gi * POOL_GROUP_DIM
        seg = proj[:, lo:lo + POOL_GROUP_DIM]
        wsum = seg
        sh = 1
        while sh < win:
            wsum = wsum + pltpu.roll(wsum, sh, 0)
            sh *= 2
        inv_cnt = 1.0 / jnp.minimum(t, float(win))
        mixed = wsum[HALO:, :] * inv_cnt - seg[HALO:, :]
        y = _dot(mixed.astype(BF16), poolw_ref[gi])
        cs = slice(gi * POOL_GROUP_DIM, (gi + 1) * POOL_GROUP_DIM)
        yp_ref[0, :, cs] = (y * pscale_ref[:, cs]).astype(BF16)


def _mixer_proj(h, wcat, conv_w, alog_pad, dtb_pad, pool_w, pool_scale):
    bsz, seq, d = h.shape
    tm = min(TOKEN_TILE, seq)
    hb = tm // HALO
    row = lambda w: pl.BlockSpec((1, tm, w), lambda b, i: (b, i, 0))
    out_f = jax.ShapeDtypeStruct((bsz, seq, DN_WIDTH), F32)
    return pl.pallas_call(
        _mixer_proj_kernel,
        grid=(bsz, seq // tm),
        in_specs=[row(d),
                  pl.BlockSpec((1, HALO, d), lambda b, i: (b, jnp.maximum(i * hb - 1, 0), 0)),
                  _const_spec(wcat.shape), _const_spec(conv_w.shape), _const_spec(alog_pad.shape),
                  _const_spec(dtb_pad.shape), _const_spec(pool_w.shape), _const_spec(pool_scale.shape)],
        out_specs=[row(DN_WIDTH), row(DN_WIDTH), row(DN_WIDTH), row(AB_PAD), row(POOL_WIDTH)],
        out_shape=[out_f, out_f, out_f,
                   jax.ShapeDtypeStruct((bsz, seq, AB_PAD), F32),
                   jax.ShapeDtypeStruct((bsz, seq, POOL_WIDTH), BF16)],
        compiler_params=_params("parallel", "parallel"),
        name="mixer_proj",
    )(h, h, wcat, conv_w, alog_pad, dtb_pad, pool_w, pool_scale)


def _delta_kernel(q_ref, k_ref, v_ref, gb_ref, o_ref, state_ref):
    @pl.when(pl.program_id(0) == 0)
    def _():
        state_ref[...] = jnp.zeros_like(state_ref)

    c_len = DELTA_CHUNK
    bsz = q_ref.shape[0]
    n_chunks = q_ref.shape[1] // c_len
    ri = lax.broadcasted_iota(jnp.int32, (c_len, c_len), 0)
    ci = lax.broadcasted_iota(jnp.int32, (c_len, c_len), 1)
    incl = ri >= ci
    eye = ri == ci
    ltri = incl.astype(F32)
    level_masks = []
    s = 1
    while s < c_len:
        level_masks.append(((ri // (2 * s)) == (ci // (2 * s))) & ((ri // s) % 2 == 1) & ((ci // s) % 2 == 0))
        s *= 2
    lane = lax.broadcasted_iota(jnp.int32, (c_len, LANES), 1)

    chains = []
    for c in range(n_chunks):
        rows = slice(c * c_len, (c + 1) * c_len)
        for b in range(bsz):
            gbc = gb_ref[b, rows, :]
            g_only = jnp.where(lane < DN_HEADS, gbc, 0.0)
            gc = jnp.dot(ltri, g_only, precision=lax.Precision.HIGHEST,
                         preferred_element_type=F32)
            gct = gc.T
            e_gc = jnp.exp(gc)
            for hh in range(DN_HEADS):
                cs = slice(hh * DN_HEAD_DIM, (hh + 1) * DN_HEAD_DIM)
                chains.append(dict(
                    c=c, b=b, idx=b * DN_HEADS + hh, rows=rows, cs=cs,
                    beta=gbc[:, DN_HEADS + hh:DN_HEADS + hh + 1],
                    gcol=gc[:, hh:hh + 1],
                    grow=gct[hh:hh + 1, :],
                    glast=gct[hh:hh + 1, c_len - 1:c_len],
                    ecol=e_gc[:, hh:hh + 1]))

    for ch in chains:
        kh = k_ref[ch["b"], ch["rows"], ch["cs"]]
        ch["decay"] = jnp.exp(jnp.where(incl, ch["gcol"] - ch["grow"], -jnp.inf))
        ch["kb"] = kh * ch["beta"]
        ch["khb"] = kh.astype(BF16)
    for ch in chains:
        ch["a"] = jnp.where(eye, 0.0, _dot_nt(ch["kb"].astype(BF16), ch["khb"]) * ch["decay"])

    for ch in chains:
        ch["dinv"] = jnp.where(eye, 1.0, 0.0) - jnp.where(level_masks[0], ch["a"], 0.0)
    for m in level_masks[1:]:
        for ch in chains:
            ch["db"] = ch["dinv"].astype(BF16)
            ch["left"] = _dot(ch["db"], jnp.where(m, ch["a"], 0.0).astype(BF16))
        for ch in chains:
            ch["dinv"] = ch["dinv"] - _dot(ch["left"].astype(BF16), ch["db"])

    for ch in chains:
        b, rows, cs = ch["b"], ch["rows"], ch["cs"]
        t_low = jnp.where(eye, 0.0, ch["dinv"]).astype(BF16)
        rhs = jnp.concatenate([v_ref[b, rows, cs] * ch["beta"], ch["kb"] * ch["ecol"]], axis=1)
        sol = rhs + _dot(t_low, rhs.astype(BF16))
        ch["u"] = sol[:, :DN_HEAD_DIM]
        qh = q_ref[b, rows, cs]
        ch["wq"] = jnp.concatenate([sol[:, DN_HEAD_DIM:].astype(BF16), (qh * ch["ecol"]).astype(BF16)], axis=0)
        ch["attn"] = (_dot_nt(qh.astype(BF16), ch["khb"]) * ch["decay"]).astype(BF16)
        ch["kdt"] = (k_ref[b, rows, cs].T * jnp.exp(ch["glast"] - ch["grow"])).astype(BF16)

    for c in range(n_chunks):
        group = [ch for ch in chains if ch["c"] == c]
        for ch in group:
            ch["proj"] = _dot(ch["wq"], state_ref[ch["idx"]].astype(BF16))
        for ch in group:
            ch["v_new"] = (ch["u"] - ch["proj"][:c_len]).astype(BF16)
        for ch in group:
            o_ref[ch["b"], ch["rows"], ch["cs"]] = ch["proj"][c_len:] + _dot(ch["attn"], ch["v_new"])
        for ch in group:
            idx = ch["idx"]
            state_ref[idx] = state_ref[idx] * jnp.exp(ch["glast"]) + _dot(ch["kdt"], ch["v_new"])


def _delta_rule(q, k, v, gb):
    bsz, seq, width = q.shape
    rows = min(DELTA_STEP_CHUNKS * DELTA_CHUNK, seq)
    blk = lambda w: pl.BlockSpec((bsz, rows, w), lambda n: (0, n, 0))
    return pl.pallas_call(
        _delta_kernel,
        grid=(seq // rows,),
        in_specs=[blk(width), blk(width), blk(width), blk(AB_PAD)],
        out_specs=blk(width),
        out_shape=jax.ShapeDtypeStruct((bsz, seq, width), F32),
        scratch_shapes=[pltpu.VMEM((bsz * DN_HEADS, DN_HEAD_DIM, DN_HEAD_DIM), F32)],
        compiler_params=_params("arbitrary"),
        name="delta_rule",
    )(q, k, v, gb)


def _mixer_out_kernel(h_ref, o_ref, yp_ref, wz_ref, wgd_ref, wgp_ref, wdn_ref, wpool_ref, wmix_ref,
                      dnw_ref, g_ref, b_ref, out_ref):
    h = h_ref[...]
    hb = h.astype(BF16)
    z = _dot(hb, wz_ref[...])
    o = o_ref[...]
    gated = []
    for hh in range(DN_HEADS):
        cs = slice(hh * DN_HEAD_DIM, (hh + 1) * DN_HEAD_DIM)
        oh = o[:, cs]
        oh = oh * lax.rsqrt(jnp.mean(oh * oh, axis=-1, keepdims=True) + RMS_EPS)
        gated.append((oh * dnw_ref[...] * _silu(z[:, cs])).astype(BF16))
    y_dn = _dot(jnp.concatenate(gated, axis=1), wdn_ref[...])
    y_pool = _dot(yp_ref[...], wpool_ref[...])
    merged = (jax.nn.sigmoid(_dot(hb, wgd_ref[...])) * y_dn
              + jax.nn.sigmoid(_dot(hb, wgp_ref[...])) * y_pool)
    mix = _dot(merged.astype(BF16), wmix_ref[...])
    out_ref[...] = _layernorm(ALPHA * h + mix, g_ref[...], b_ref[...])


def _mixer_out(h2d, o2d, yp2d, wz, wgd, wgp, wdn, wpool, wmix, dnw, g, b):
    t, d = h2d.shape
    tm = min(TOKEN_TILE, t)
    row = lambda w: pl.BlockSpec((tm, w), lambda i: (i, 0))
    consts = (wz, wgd, wgp, wdn, wpool, wmix, dnw, g, b)
    return pl.pallas_call(
        _mixer_out_kernel,
        grid=(t // tm,),
        in_specs=[row(d), row(DN_WIDTH), row(POOL_WIDTH)] + [_const_spec(c.shape) for c in consts],
        out_specs=row(d),
        out_shape=jax.ShapeDtypeStruct((t, d), F32),
        compiler_params=_params("parallel"),
        name="mixer_out",
    )(h2d, o2d, yp2d, *consts)


def _mem_kv_kernel(m_ref, g_ref, b_ref, wk_ref, wv_ref, k_ref, v_ref):
    m = _layernorm(m_ref[0], g_ref[...], b_ref[...]).astype(BF16)
    k_ref[0] = _dot(m, wk_ref[...]).astype(BF16)
    v_ref[0] = _dot(m, wv_ref[...]).astype(BF16)


def _mem_kv(mem, g, b, wk, wv):
    bsz, n_mem, d = mem.shape
    blk = pl.BlockSpec((1, n_mem, d), lambda i: (i, 0, 0))
    out = jax.ShapeDtypeStruct((bsz, n_mem, d), BF16)
    return pl.pallas_call(
        _mem_kv_kernel,
        grid=(bsz,),
        in_specs=[blk, _const_spec(g.shape), _const_spec(b.shape), _const_spec(wk.shape), _const_spec(wv.shape)],
        out_specs=[blk, blk],
        out_shape=[out, out],
        compiler_params=_params("parallel"),
        name="mem_kv",
    )(mem, g, b, wk, wv)


def _xattn_ln_kernel(x_ref, k_ref, v_ref, wq_ref, wo_ref, g_ref, b_ref, o_ref):
    x = x_ref[0]
    d = x.shape[-1]
    hd = d // XA_HEADS
    q = _dot(x.astype(BF16), wq_ref[...])
    heads = []
    for hh in range(XA_HEADS):
        cs = slice(hh * hd, (hh + 1) * hd)
        s = _dot_nt(q[:, cs].astype(BF16), k_ref[0, :, cs]) * (hd ** -0.5)
        e = jnp.exp(s - jnp.max(s, axis=-1, keepdims=True))
        pr = e * (1.0 / jnp.sum(e, axis=-1, keepdims=True))
        heads.append(_dot(pr.astype(BF16), v_ref[0, :, cs]).astype(BF16))
    xa = _dot(jnp.concatenate(heads, axis=1), wo_ref[...])
    o_ref[0] = _layernorm(ALPHA * x + xa, g_ref[...], b_ref[...])


def _xattn_ln(x, k, v, wq, wo, g, b):
    bsz, seq, d = x.shape
    n_mem = k.shape[1]
    tm = min(TOKEN_TILE, seq)
    row = pl.BlockSpec((1, tm, d), lambda bb, i: (bb, i, 0))
    kv = pl.BlockSpec((1, n_mem, d), lambda bb, i: (bb, 0, 0))
    return pl.pallas_call(
        _xattn_ln_kernel,
        grid=(bsz, seq // tm),
        in_specs=[row, kv, kv, _const_spec(wq.shape), _const_spec(wo.shape),
                  _const_spec(g.shape), _const_spec(b.shape)],
        out_specs=row,
        out_shape=jax.ShapeDtypeStruct((bsz, seq, d), F32),
        compiler_params=_params("parallel", "parallel"),
        name="xattn_ln",
    )(x, k, v, wq, wo, g, b)


def _row(vec):
    return vec.reshape(1, -1).astype(F32)


def _pad_lanes(vec):
    return jnp.zeros((1, AB_PAD), F32).at[0, :vec.shape[0]].set(vec.astype(F32))


def kernel(x, mem, ffn1_w_gate, ffn1_w_up, ffn1_w_down, ln1_g, ln1_b, w_in, conv_w, a_log, dt_bias, dn_norm_w, w_dn_branch, pool_w, pool_scale, w_pool_branch, w_mix_out, ln2_g, ln2_b, mem_ln_g, mem_ln_b, xa_wq, xa_wk, xa_wv, xa_wo, ln3_g, ln3_b, ffn2_w_gate, ffn2_w_up, ffn2_w_down, ln4_g, ln4_b):
    bsz, seq, d = x.shape
    t = bsz * seq
    bf = lambda w: w.astype(BF16)
    o_qkv = 3 * DN_WIDTH
    o_z = o_qkv + DN_WIDTH
    o_ab = o_z + 2 * DN_HEADS
    o_pool = o_ab + POOL_WIDTH
    o_gdn = o_pool + d

    for l in range(ffn1_w_gate.shape[0]):
        wi = w_in[l]
        later = [ffn2_w_gate[l], ffn2_w_up[l], ffn2_w_down[l], wi[:, o_qkv:o_z], wi[:, o_pool:o_gdn], wi[:, o_gdn:],
                 w_dn_branch[l], w_pool_branch[l], w_mix_out[l], xa_wq[l], xa_wk[l], xa_wv[l], xa_wo[l]]
        x2d, later = _ffn_ln(x.reshape(t, d), bf(ffn1_w_gate[l]), bf(ffn1_w_up[l]), bf(ffn1_w_down[l]),
                             _row(ln1_g[l]), _row(ln1_b[l]), riders=later)
        f2_gate, f2_up, f2_down, w_z, w_gdn, w_gpool, w_dn, w_pool, w_mix, wq, wk, wv, wo = later

        w_ab = jnp.zeros((d, AB_PAD), F32).at[:, :2 * DN_HEADS].set(wi[:, o_z:o_ab])
        wcat = bf(jnp.concatenate([wi[:, :o_qkv], wi[:, o_ab:o_pool], w_ab], axis=1))
        q, k, v, gb, ypool = _mixer_proj(x2d.reshape(bsz, seq, d), wcat, conv_w[l].astype(F32),
                                         _pad_lanes(a_log[l]), _pad_lanes(dt_bias[l]),
                                         bf(pool_w[l]), _row(pool_scale[l]))
        o = _delta_rule(q, k, v, gb)
        x2d = _mixer_out(x2d, o.reshape(t, DN_WIDTH), ypool.reshape(t, POOL_WIDTH),
                         w_z, w_gdn, w_gpool, w_dn, w_pool, w_mix,
                         _row(dn_norm_w[l]), _row(ln2_g[l]), _row(ln2_b[l]))

        mk, mv = _mem_kv(mem, _row(mem_ln_g[l]), _row(mem_ln_b[l]), wk, wv)
        x3 = _xattn_ln(x2d.reshape(bsz, seq, d), mk, mv, wq, wo, _row(ln3_g[l]), _row(ln3_b[l]))
        x, _ = _ffn_ln(x3.reshape(t, d), f2_gate, f2_up, f2_down, _row(ln4_g[l]), _row(ln4_b[l]))
        x = x.reshape(bsz, seq, d)
    return x
```

```python
import functools

import jax
import jax.numpy as jnp
from jax import lax
from jax.experimental import pallas as pl
from jax.experimental.pallas import tpu as pltpu

F32 = jnp.float32
BF16 = jnp.bfloat16

DN_HEADS = 4
DN_HEAD_DIM = 128
DN_WIDTH = DN_HEADS * DN_HEAD_DIM
CONV_K = 4
POOL_WINDOWS = (2, 4, 8, 16)
POOL_GROUP_DIM = 128
POOL_WIDTH = len(POOL_WINDOWS) * POOL_GROUP_DIM
XA_HEADS = 4
LN_EPS = 1e-5
RMS_EPS = 1e-6
L2_EPS = 1e-6
DEPTH = 1
ALPHA = (2.0 * DEPTH) ** 0.25

LANES = 128
BF16_ROWS = 16
HALO = 16
DELTA_CHUNK = 128
AB_PAD = LANES
VMEM_LIMIT = 60000 * 1024

TOKEN_TILE = 1024
DELTA_STEP_CHUNKS = 2
MXU_DIM = 256
FF_SLICE = 3 * MXU_DIM


def _dot(a, b):
    return jnp.dot(a, b, preferred_element_type=F32)


def _dot_nt(a, b):
    return lax.dot_general(a, b, (((1,), (1,)), ((), ())), preferred_element_type=F32)


def _layernorm(y, g, b):
    mu = jnp.mean(y, axis=-1, keepdims=True)
    yc = y - mu
    var = jnp.mean(yc * yc, axis=-1, keepdims=True)
    return yc * lax.rsqrt(var + LN_EPS) * g + b


def _silu(x):
    return x * jax.nn.sigmoid(x)


def _const_spec(shape):
    nd = len(shape)
    return pl.BlockSpec(shape, lambda *_: (0,) * nd, pipeline_mode=pl.Buffered(1))


def _params(*sem):
    return pltpu.CompilerParams(dimension_semantics=sem, vmem_limit_bytes=VMEM_LIMIT)


def _split_w_in(blk):
    o_qkv = 3 * DN_WIDTH
    o_z = o_qkv + DN_WIDTH
    d = (blk.shape[1] - o_z - 2 * DN_HEADS - POOL_WIDTH) // 2
    ab = blk[:, o_z:o_z + AB_PAD]
    lane = lax.broadcasted_iota(jnp.int32, ab.shape, 1)
    w_ab = jnp.where(lane < 2 * DN_HEADS, ab, 0.0)
    rest = blk[:, o_z + 2 * DN_HEADS:]
    wcat = jnp.concatenate([blk[:, :o_qkv], rest[:, :POOL_WIDTH], w_ab], axis=1)
    parts = (wcat, blk[:, o_qkv:o_z], rest[:, POOL_WIDTH:POOL_WIDTH + d], rest[:, POOL_WIDTH + d:])
    return [p.astype(BF16) for p in parts]


def _ffn_ln_kernel(n_riders, split_w_in, x_ref, wg_ref, wu_ref, wd_ref, g_ref, b_ref, *refs):
    n_in = n_riders + int(split_w_in)
    rider_in, o_ref, rider_out = refs[:n_in], refs[n_in], refs[n_in + 1:]
    for src, dst in zip(rider_in[:n_riders], rider_out[:n_riders]):
        dst[...] = src[...].astype(BF16)
    if split_w_in:
        for part, dst in zip(_split_w_in(rider_in[n_riders][...]), rider_out[n_riders:]):
            dst[...] = part

    x = x_ref[...]
    xb = x.astype(BF16)
    ff = wg_ref.shape[1]
    y = None
    for lo in range(0, ff, FF_SLICE):
        cs = slice(lo, min(lo + FF_SLICE, ff))
        hg = _dot(xb, wg_ref[:, cs])
        hu = _dot(xb, wu_ref[:, cs])
        part = _dot((_silu(hg) * hu).astype(BF16), wd_ref[cs, :])
        y = part if y is None else y + part
    o_ref[...] = _layernorm(ALPHA * x + 0.5 * y, g_ref[...], b_ref[...])


def _ffn_ln(x2d, wg, wu, wd, g, b, riders=(), w_in=None):
    t, d = x2d.shape
    ff = wg.shape[1]
    tm = min(TOKEN_TILE, t)
    steps = t // tm
    row = pl.BlockSpec((tm, d), lambda i: (i, 0))
    rider_specs = []
    for w in riders:
        rows = w.shape[0] // steps
        assert w.shape[0] == rows * steps and rows % BF16_ROWS == 0, w.shape
        rider_specs.append(pl.BlockSpec((rows, w.shape[1]), lambda i: (i, 0)))
    in_specs, out_specs = list(rider_specs), list(rider_specs)
    out_shape = [jax.ShapeDtypeStruct(w.shape, BF16) for w in riders]
    operands = list(riders)
    if w_in is not None:
        rows = w_in.shape[0] // steps
        assert w_in.shape[0] == rows * steps and rows % BF16_ROWS == 0, w_in.shape
        in_specs.append(pl.BlockSpec((rows, w_in.shape[1]), lambda i: (i, 0)))
        operands.append(w_in)
        for width in (3 * DN_WIDTH + POOL_WIDTH + AB_PAD, DN_WIDTH, d, d):
            out_specs.append(pl.BlockSpec((rows, width), lambda i: (i, 0)))
            out_shape.append(jax.ShapeDtypeStruct((w_in.shape[0], width), BF16))
    outs = pl.pallas_call(
        functools.partial(_ffn_ln_kernel, len(riders), w_in is not None),
        grid=(steps,),
        in_specs=[row, _const_spec((d, ff)), _const_spec((d, ff)), _const_spec((ff, d)),
                  _const_spec((1, d)), _const_spec((1, d))] + in_specs,
        out_specs=[row] + out_specs,
        out_shape=[jax.ShapeDtypeStruct((t, d), F32)] + out_shape,
        compiler_params=_params("parallel"),
        name="ffn_ln",
    )(x2d, wg, wu, wd, g, b, *operands)
    return outs[0], outs[1:]


def _mixer_proj_kernel(h_ref, halo_ref, wcat_ref, convw_ref, alog_ref, dtb_ref, poolw_ref,
                       pscale_ref, q_ref, k_ref, v_ref, gb_ref, yp_ref):
    i = pl.program_id(1)
    tm = h_ref.shape[1]
    qkv_w = 3 * DN_WIDTH
    halo = jnp.where(i == 0, 0.0, halo_ref[0])
    hcat = jnp.concatenate([halo, h_ref[0]], axis=0).astype(BF16)
    proj = _dot(hcat, wcat_ref[...])

    qkv_pre = proj[:, :qkv_w]
    conv = qkv_pre * convw_ref[CONV_K - 1:CONV_K, :]
    for s in range(1, CONV_K):
        conv = conv + pltpu.roll(qkv_pre, s, 0) * convw_ref[CONV_K - 1 - s:CONV_K - s, :]
    qkv = _silu(conv[HALO:, :])

    for part, ref, scale in ((0, q_ref, DN_HEAD_DIM ** -0.5), (1, k_ref, None), (2, v_ref, None)):
        for hh in range(DN_HEADS):
            lo = part * DN_WIDTH + hh * DN_HEAD_DIM
            seg = qkv[:, lo:lo + DN_HEAD_DIM]
            if part < 2:
                seg = seg * lax.rsqrt(jnp.sum(seg * seg, axis=-1, keepdims=True) + L2_EPS)
            if scale is not None:
                seg = seg * scale
            ref[0, :, hh * DN_HEAD_DIM:(hh + 1) * DN_HEAD_DIM] = seg

    ab = proj[HALO:, qkv_w + POOL_WIDTH:]
    xa = ab + dtb_ref[...]
    softplus = jnp.maximum(xa, 0.0) + jnp.log1p(jnp.exp(-jnp.abs(xa)))
    gval = -jnp.exp(alog_ref[...]) * softplus
    lane = lax.broadcasted_iota(jnp.int32, ab.shape, 1)
    gb_ref[0] = jnp.where(lane < DN_HEADS, gval, jax.nn.sigmoid(ab))

    t = (i * tm + lax.broadcasted_iota(jnp.int32, (tm, 1), 0) + 1).astype(F32)
    for gi, win in enumerate(POOL_WINDOWS):
        lo = qkv_w + gi * POOL_GROUP_DIM
        seg = proj[:, lo:lo + POOL_GROUP_DIM]
        wsum = seg
        sh = 1
        while sh < win:
            wsum = wsum + pltpu.roll(wsum, sh, 0)
            sh *= 2
        inv_cnt = 1.0 / jnp.minimum(t, float(win))
        mixed = wsum[HALO:, :] * inv_cnt - seg[HALO:, :]
        y = _dot(mixed.astype(BF16), poolw_ref[gi])
        cs = slice(gi * POOL_GROUP_DIM, (gi + 1) * POOL_GROUP_DIM)
        yp_ref[0, :, cs] = (y * pscale_ref[:, cs]).astype(BF16)


def _mixer_proj(h, wcat, conv_w, alog_pad, dtb_pad, pool_w, pool_scale):
    bsz, seq, d = h.shape
    tm = min(TOKEN_TILE, seq)
    hb = tm // HALO
    row = lambda w: pl.BlockSpec((1, tm, w), lambda b, i: (b, i, 0))
    out_f = jax.ShapeDtypeStruct((bsz, seq, DN_WIDTH), F32)
    return pl.pallas_call(
        _mixer_proj_kernel,
        grid=(bsz, seq // tm),
        in_specs=[row(d),
                  pl.BlockSpec((1, HALO, d), lambda b, i: (b, jnp.maximum(i * hb - 1, 0), 0)),
                  _const_spec(wcat.shape), _const_spec(conv_w.shape), _const_spec(alog_pad.shape),
                  _const_spec(dtb_pad.shape), _const_spec(pool_w.shape), _const_spec(pool_scale.shape)],
        out_specs=[row(DN_WIDTH), row(DN_WIDTH), row(DN_WIDTH), row(AB_PAD), row(POOL_WIDTH)],
        out_shape=[out_f, out_f, out_f,
                   jax.ShapeDtypeStruct((bsz, seq, AB_PAD), F32),
                   jax.ShapeDtypeStruct((bsz, seq, POOL_WIDTH), BF16)],
        compiler_params=_params("parallel", "parallel"),
        name="mixer_proj",
    )(h, h, wcat, conv_w, alog_pad, dtb_pad, pool_w, pool_scale)


def _delta_kernel(q_ref, k_ref, v_ref, gb_ref, o_ref, state_ref):
    @pl.when(pl.program_id(0) == 0)
    def _():
        state_ref[...] = jnp.zeros_like(state_ref)

    c_len = DELTA_CHUNK
    bsz = q_ref.shape[0]
    n_chunks = q_ref.shape[1] // c_len
    ri = lax.broadcasted_iota(jnp.int32, (c_len, c_len), 0)
    ci = lax.broadcasted_iota(jnp.int32, (c_len, c_len), 1)
    incl = ri >= ci
    eye = ri == ci
    ltri = incl.astype(F32)
    level_masks = []
    s = 1
    while s < c_len:
        level_masks.append(((ri // (2 * s)) == (ci // (2 * s))) & ((ri // s) % 2 == 1) & ((ci // s) % 2 == 0))
        s *= 2
    lane = lax.broadcasted_iota(jnp.int32, (c_len, LANES), 1)

    chains = []
    for c in range(n_chunks):
        rows = slice(c * c_len, (c + 1) * c_len)
        for b in range(bsz):
            gbc = gb_ref[b, rows, :]
            g_only = jnp.where(lane < DN_HEADS, gbc, 0.0)
            gc = jnp.dot(ltri, g_only, precision=lax.Precision.HIGHEST,
                         preferred_element_type=F32)
            gct = gc.T
            e_gc = jnp.exp(gc)
            for hh in range(DN_HEADS):
                cs = slice(hh * DN_HEAD_DIM, (hh + 1) * DN_HEAD_DIM)
                chains.append(dict(
                    c=c, b=b, idx=b * DN_HEADS + hh, rows=rows, cs=cs,
                    beta=gbc[:, DN_HEADS + hh:DN_HEADS + hh + 1],
                    gcol=gc[:, hh:hh + 1],
                    grow=gct[hh:hh + 1, :],
                    glast=gct[hh:hh + 1, c_len - 1:c_len],
                    ecol=e_gc[:, hh:hh + 1]))

    for ch in chains:
        kh = k_ref[ch["b"], ch["rows"], ch["cs"]]
        ch["decay"] = jnp.exp(jnp.where(incl, ch["gcol"] - ch["grow"], -jnp.inf))
        ch["kb"] = kh * ch["beta"]
        ch["khb"] = kh.astype(BF16)
    for ch in chains:
        ch["a"] = jnp.where(eye, 0.0, _dot_nt(ch["kb"].astype(BF16), ch["khb"]) * ch["decay"])

    for ch in chains:
        ch["dinv"] = jnp.where(eye, 1.0, 0.0) - jnp.where(level_masks[0], ch["a"], 0.0)
    for m in level_masks[1:]:
        for ch in chains:
            ch["db"] = ch["dinv"].astype(BF16)
            ch["left"] = _dot(ch["db"], jnp.where(m, ch["a"], 0.0).astype(BF16))
        for ch in chains:
            ch["dinv"] = ch["dinv"] - _dot(ch["left"].astype(BF16), ch["db"])

    for ch in chains:
        b, rows, cs = ch["b"], ch["rows"], ch["cs"]
        t_low = jnp.where(eye, 0.0, ch["dinv"]).astype(BF16)
        rhs = jnp.concatenate([v_ref[b, rows, cs] * ch["beta"], ch["kb"] * ch["ecol"]], axis=1)
        sol = rhs + _dot(t_low, rhs.astype(BF16))
        ch["u"] = sol[:, :DN_HEAD_DIM]
        qh = q_ref[b, rows, cs]
        ch["wq"] = jnp.concatenate([sol[:, DN_HEAD_DIM:].astype(BF16), (qh * ch["ecol"]).astype(BF16)], axis=0)
        ch["attn"] = (_dot_nt(qh.astype(BF16), ch["khb"]) * ch["decay"]).astype(BF16)
        ch["kdt"] = (k_ref[b, rows, cs].T * jnp.exp(ch["glast"] - ch["grow"])).astype(BF16)

    for c in range(n_chunks):
        group = [ch for ch in chains if ch["c"] == c]
        for ch in group:
            ch["proj"] = _dot(ch["wq"], state_ref[ch["idx"]].astype(BF16))
        for ch in group:
            ch["v_new"] = (ch["u"] - ch["proj"][:c_len]).astype(BF16)
        for ch in group:
            o_ref[ch["b"], ch["rows"], ch["cs"]] = ch["proj"][c_len:] + _dot(ch["attn"], ch["v_new"])
        for ch in group:
            idx = ch["idx"]
            state_ref[idx] = state_ref[idx] * jnp.exp(ch["glast"]) + _dot(ch["kdt"], ch["v_new"])


def _delta_rule(q, k, v, gb):
    bsz, seq, width = q.shape
    rows = min(DELTA_STEP_CHUNKS * DELTA_CHUNK, seq)
    blk = lambda w: pl.BlockSpec((bsz, rows, w), lambda n: (0, n, 0))
    return pl.pallas_call(
        _delta_kernel,
        grid=(seq // rows,),
        in_specs=[blk(width), blk(width), blk(width), blk(AB_PAD)],
        out_specs=blk(width),
        out_shape=jax.ShapeDtypeStruct((bsz, seq, width), F32),
        scratch_shapes=[pltpu.VMEM((bsz * DN_HEADS, DN_HEAD_DIM, DN_HEAD_DIM), F32)],
        compiler_params=_params("arbitrary"),
        name="delta_rule",
    )(q, k, v, gb)


def _mixer_out_kernel(h_ref, o_ref, yp_ref, wz_ref, wgd_ref, wgp_ref, wdn_ref, wpool_ref, wmix_ref,
                      dnw_ref, g_ref, b_ref, out_ref):
    h = h_ref[...]
    hb = h.astype(BF16)
    z = _dot(hb, wz_ref[...])
    o = o_ref[...]
    gated = []
    for hh in range(DN_HEADS):
        cs = slice(hh * DN_HEAD_DIM, (hh + 1) * DN_HEAD_DIM)
        oh = o[:, cs]
        oh = oh * lax.rsqrt(jnp.mean(oh * oh, axis=-1, keepdims=True) + RMS_EPS)
        gated.append((oh * dnw_ref[...] * _silu(z[:, cs])).astype(BF16))
    y_dn = _dot(jnp.concatenate(gated, axis=1), wdn_ref[...])
    y_pool = _dot(yp_ref[...], wpool_ref[...])
    merged = (jax.nn.sigmoid(_dot(hb, wgd_ref[...])) * y_dn
              + jax.nn.sigmoid(_dot(hb, wgp_ref[...])) * y_pool)
    mix = _dot(merged.astype(BF16), wmix_ref[...])
    out_ref[...] = _layernorm(ALPHA * h + mix, g_ref[...], b_ref[...])


def _mixer_out(h2d, o2d, yp2d, wz, wgd, wgp, wdn, wpool, wmix, dnw, g, b):
    t, d = h2d.shape
    tm = min(TOKEN_TILE, t)
    row = lambda w: pl.BlockSpec((tm, w), lambda i: (i, 0))
    consts = (wz, wgd, wgp, wdn, wpool, wmix, dnw, g, b)
    return pl.pallas_call(
        _mixer_out_kernel,
        grid=(t // tm,),
        in_specs=[row(d), row(DN_WIDTH), row(POOL_WIDTH)] + [_const_spec(c.shape) for c in consts],
        out_specs=row(d),
        out_shape=jax.ShapeDtypeStruct((t, d), F32),
        compiler_params=_params("parallel"),
        name="mixer_out",
    )(h2d, o2d, yp2d, *consts)


def _mem_kv_kernel(m_ref, g_ref, b_ref, wk_ref, wv_ref, k_ref, v_ref):
    m = _layernorm(m_ref[0], g_ref[...], b_ref[...]).astype(BF16)
    k_ref[0] = _dot(m, wk_ref[...]).astype(BF16)
    v_ref[0] = _dot(m, wv_ref[...]).astype(BF16)


def _mem_kv(mem, g, b, wk, wv):
    bsz, n_mem, d = mem.shape
    blk = pl.BlockSpec((1, n_mem, d), lambda i: (i, 0, 0))
    out = jax.ShapeDtypeStruct((bsz, n_mem, d), BF16)
    return pl.pallas_call(
        _mem_kv_kernel,
        grid=(bsz,),
        in_specs=[blk, _const_spec(g.shape), _const_spec(b.shape), _const_spec(wk.shape), _const_spec(wv.shape)],
        out_specs=[blk, blk],
        out_shape=[out, out],
        compiler_params=_params("parallel"),
        name="mem_kv",
    )(mem, g, b, wk, wv)


def _xattn_ln_kernel(x_ref, k_ref, v_ref, wq_ref, wo_ref, g_ref, b_ref, o_ref):
    x = x_ref[0]
    d = x.shape[-1]
    hd = d // XA_HEADS
    q = _dot(x.astype(BF16), wq_ref[...])
    heads = []
    for hh in range(XA_HEADS):
        cs = slice(hh * hd, (hh + 1) * hd)
        s = _dot_nt(q[:, cs].astype(BF16), k_ref[0, :, cs]) * (hd ** -0.5)
        e = jnp.exp(s - jnp.max(s, axis=-1, keepdims=True))
        pr = e * (1.0 / jnp.sum(e, axis=-1, keepdims=True))
        heads.append(_dot(pr.astype(BF16), v_ref[0, :, cs]).astype(BF16))
    xa = _dot(jnp.concatenate(heads, axis=1), wo_ref[...])
    o_ref[0] = _layernorm(ALPHA * x + xa, g_ref[...], b_ref[...])


def _xattn_ln(x, k, v, wq, wo, g, b):
    bsz, seq, d = x.shape
    n_mem = k.shape[1]
    tm = min(TOKEN_TILE, seq)
    row = pl.BlockSpec((1, tm, d), lambda bb, i: (bb, i, 0))
    kv = pl.BlockSpec((1, n_mem, d), lambda bb, i: (bb, 0, 0))
    return pl.pallas_call(
        _xattn_ln_kernel,
        grid=(bsz, seq // tm),
        in_specs=[row, kv, kv, _const_spec(wq.shape), _const_spec(wo.shape),
                  _const_spec(g.shape), _const_spec(b.shape)],
        out_specs=row,
        out_shape=jax.ShapeDtypeStruct((bsz, seq, d), F32),
        compiler_params=_params("parallel", "parallel"),
        name="xattn_ln",
    )(x, k, v, wq, wo, g, b)


def _row(vec):
    return vec.reshape(1, -1).astype(F32)


def _pad_lanes(vec):
    return jnp.zeros((1, AB_PAD), F32).at[0, :vec.shape[0]].set(vec.astype(F32))


def kernel(x, mem, ffn1_w_gate, ffn1_w_up, ffn1_w_down, ln1_g, ln1_b, w_in, conv_w, a_log, dt_bias, dn_norm_w, w_dn_branch, pool_w, pool_scale, w_pool_branch, w_mix_out, ln2_g, ln2_b, mem_ln_g, mem_ln_b, xa_wq, xa_wk, xa_wv, xa_wo, ln3_g, ln3_b, ffn2_w_gate, ffn2_w_up, ffn2_w_down, ln4_g, ln4_b):
    bsz, seq, d = x.shape
    t = bsz * seq
    bf = lambda w: w.astype(BF16)

    for l in range(ffn1_w_gate.shape[0]):
        later = [ffn2_w_gate[l], ffn2_w_up[l], ffn2_w_down[l], w_dn_branch[l], w_pool_branch[l], w_mix_out[l],
                 xa_wq[l], xa_wk[l], xa_wv[l], xa_wo[l]]
        x2d, later = _ffn_ln(x.reshape(t, d), bf(ffn1_w_gate[l]), bf(ffn1_w_up[l]), bf(ffn1_w_down[l]),
                             _row(ln1_g[l]), _row(ln1_b[l]), riders=later, w_in=w_in[l])
        f2_gate, f2_up, f2_down, w_dn, w_pool, w_mix, wq, wk, wv, wo, wcat, w_z, w_gdn, w_gpool = later

        q, k, v, gb, ypool = _mixer_proj(x2d.reshape(bsz, seq, d), wcat, conv_w[l].astype(F32),
                                         _pad_lanes(a_log[l]), _pad_lanes(dt_bias[l]),
                                         bf(pool_w[l]), _row(pool_scale[l]))
        o = _delta_rule(q, k, v, gb)
        x2d = _mixer_out(x2d, o.reshape(t, DN_WIDTH), ypool.reshape(t, POOL_WIDTH),
                         w_z, w_gdn, w_gpool, w_dn, w_pool, w_mix,
                         _row(dn_norm_w[l]), _row(ln2_g[l]), _row(ln2_b[l]))

        mk, mv = _mem_kv(mem, _row(mem_ln_g[l]), _row(mem_ln_b[l]), wk, wv)
        x3 = _xattn_ln(x2d.reshape(bsz, seq, d), mk, mv, wq, wo, _row(ln3_g[l]), _row(ln3_b[l]))
        x, _ = _ffn_ln(x3.reshape(t, d), f2_gate, f2_up, f2_down, _row(ln4_g[l]), _row(ln4_b[l]))
        x = x.reshape(bsz, seq, d)
    return x
```

```python
import functools

import jax
import jax.numpy as jnp
from jax import lax
from jax.experimental import pallas as pl
from jax.experimental.pallas import tpu as pltpu

F32 = jnp.float32
BF16 = jnp.bfloat16

DN_HEADS = 4
DN_HEAD_DIM = 128
DN_WIDTH = DN_HEADS * DN_HEAD_DIM
CONV_K = 4
POOL_WINDOWS = (2, 4, 8, 16)
POOL_GROUP_DIM = 128
POOL_WIDTH = len(POOL_WINDOWS) * POOL_GROUP_DIM
XA_HEADS = 4
LN_EPS = 1e-5
RMS_EPS = 1e-6
L2_EPS = 1e-6
DEPTH = 1
ALPHA = (2.0 * DEPTH) ** 0.25

LANES = 128
BF16_ROWS = 16
HALO = 16
DELTA_CHUNK = 128
AB_PAD = LANES
VMEM_LIMIT = 60000 * 1024

TOKEN_TILE = 1024
DELTA_STEP_CHUNKS = 2
ROW_GROUP = 256
MXU_DIM = 256
FF_SLICE = 3 * MXU_DIM


def _dot(a, b):
    return jnp.dot(a, b, preferred_element_type=F32)


def _dot_nt(a, b):
    return lax.dot_general(a, b, (((1,), (1,)), ((), ())), preferred_element_type=F32)


def _layernorm(y, g, b):
    mu = jnp.mean(y, axis=-1, keepdims=True)
    yc = y - mu
    var = jnp.mean(yc * yc, axis=-1, keepdims=True)
    return yc * lax.rsqrt(var + LN_EPS) * g + b


def _silu(x):
    return x * jax.nn.sigmoid(x)


def _row_groups(n_rows):
    size = min(ROW_GROUP, n_rows)
    return [slice(lo, lo + size) for lo in range(0, n_rows, size)]


def _const_spec(shape):
    nd = len(shape)
    return pl.BlockSpec(shape, lambda *_: (0,) * nd, pipeline_mode=pl.Buffered(1))


def _params(*sem):
    return pltpu.CompilerParams(dimension_semantics=sem, vmem_limit_bytes=VMEM_LIMIT)


def _split_w_in(blk):
    o_qkv = 3 * DN_WIDTH
    o_z = o_qkv + DN_WIDTH
    d = (blk.shape[1] - o_z - 2 * DN_HEADS - POOL_WIDTH) // 2
    ab = blk[:, o_z:o_z + AB_PAD]
    lane = lax.broadcasted_iota(jnp.int32, ab.shape, 1)
    w_ab = jnp.where(lane < 2 * DN_HEADS, ab, 0.0)
    rest = blk[:, o_z + 2 * DN_HEADS:]
    wcat = jnp.concatenate([blk[:, :o_qkv], rest[:, :POOL_WIDTH], w_ab], axis=1)
    parts = (wcat, blk[:, o_qkv:o_z], rest[:, POOL_WIDTH:POOL_WIDTH + d], rest[:, POOL_WIDTH + d:])
    return [p.astype(BF16) for p in parts]


def _ffn_ln_kernel(n_riders, split_w_in, x_ref, wg_ref, wu_ref, wd_ref, g_ref, b_ref, *refs):
    n_in = n_riders + int(split_w_in)
    rider_in, o_ref, rider_out = refs[:n_in], refs[n_in], refs[n_in + 1:]
    for src, dst in zip(rider_in[:n_riders], rider_out[:n_riders]):
        dst[...] = src[...].astype(BF16)
    if split_w_in:
        for part, dst in zip(_split_w_in(rider_in[n_riders][...]), rider_out[n_riders:]):
            dst[...] = part

    ff = wg_ref.shape[1]
    for rows in _row_groups(x_ref.shape[0]):
        x = x_ref[rows, :]
        xb = x.astype(BF16)
        y = None
        for lo in range(0, ff, FF_SLICE):
            cs = slice(lo, min(lo + FF_SLICE, ff))
            hg = _dot(xb, wg_ref[:, cs])
            hu = _dot(xb, wu_ref[:, cs])
            part = _dot((_silu(hg) * hu).astype(BF16), wd_ref[cs, :])
            y = part if y is None else y + part
        o_ref[rows, :] = _layernorm(ALPHA * x + 0.5 * y, g_ref[...], b_ref[...])


def _ffn_ln(x2d, wg, wu, wd, g, b, riders=(), w_in=None):
    t, d = x2d.shape
    ff = wg.shape[1]
    tm = min(TOKEN_TILE, t)
    steps = t // tm
    row = pl.BlockSpec((tm, d), lambda i: (i, 0))
    rider_specs = []
    for w in riders:
        rows = w.shape[0] // steps
        assert w.shape[0] == rows * steps and rows % BF16_ROWS == 0, w.shape
        rider_specs.append(pl.BlockSpec((rows, w.shape[1]), lambda i: (i, 0)))
    in_specs, out_specs = list(rider_specs), list(rider_specs)
    out_shape = [jax.ShapeDtypeStruct(w.shape, BF16) for w in riders]
    operands = list(riders)
    if w_in is not None:
        rows = w_in.shape[0] // steps
        assert w_in.shape[0] == rows * steps and rows % BF16_ROWS == 0, w_in.shape
        in_specs.append(pl.BlockSpec((rows, w_in.shape[1]), lambda i: (i, 0)))
        operands.append(w_in)
        for width in (3 * DN_WIDTH + POOL_WIDTH + AB_PAD, DN_WIDTH, d, d):
            out_specs.append(pl.BlockSpec((rows, width), lambda i: (i, 0)))
            out_shape.append(jax.ShapeDtypeStruct((w_in.shape[0], width), BF16))
    outs = pl.pallas_call(
        functools.partial(_ffn_ln_kernel, len(riders), w_in is not None),
        grid=(steps,),
        in_specs=[row, _const_spec((d, ff)), _const_spec((d, ff)), _const_spec((ff, d)),
                  _const_spec((1, d)), _const_spec((1, d))] + in_specs,
        out_specs=[row] + out_specs,
        out_shape=[jax.ShapeDtypeStruct((t, d), F32)] + out_shape,
        compiler_params=_params("parallel"),
        name="ffn_ln",
    )(x2d, wg, wu, wd, g, b, *operands)
    return outs[0], outs[1:]


def _mixer_proj_kernel(h_ref, halo_ref, wcat_ref, convw_ref, alog_ref, dtb_ref, poolw_ref,
                       pscale_ref, q_ref, k_ref, v_ref, gb_ref, yp_ref):
    i = pl.program_id(1)
    tm = h_ref.shape[1]
    qkv_w = 3 * DN_WIDTH
    halo = jnp.where(i == 0, 0.0, halo_ref[0])
    hcat = jnp.concatenate([halo, h_ref[0]], axis=0).astype(BF16)
    proj = _dot(hcat, wcat_ref[...])

    qkv_pre = proj[:, :qkv_w]
    conv = qkv_pre * convw_ref[CONV_K - 1:CONV_K, :]
    for s in range(1, CONV_K):
        conv = conv + pltpu.roll(qkv_pre, s, 0) * convw_ref[CONV_K - 1 - s:CONV_K - s, :]
    qkv = _silu(conv[HALO:, :])

    for part, ref, scale in ((0, q_ref, DN_HEAD_DIM ** -0.5), (1, k_ref, None), (2, v_ref, None)):
        for hh in range(DN_HEADS):
            lo = part * DN_WIDTH + hh * DN_HEAD_DIM
            seg = qkv[:, lo:lo + DN_HEAD_DIM]
            if part < 2:
                seg = seg * lax.rsqrt(jnp.sum(seg * seg, axis=-1, keepdims=True) + L2_EPS)
            if scale is not None:
                seg = seg * scale
            ref[0, :, hh * DN_HEAD_DIM:(hh + 1) * DN_HEAD_DIM] = seg

    ab = proj[HALO:, qkv_w + POOL_WIDTH:]
    xa = ab + dtb_ref[...]
    softplus = jnp.maximum(xa, 0.0) + jnp.log1p(jnp.exp(-jnp.abs(xa)))
    gval = -jnp.exp(alog_ref[...]) * softplus
    lane = lax.broadcasted_iota(jnp.int32, ab.shape, 1)
    gb_ref[0] = jnp.where(lane < DN_HEADS, gval, jax.nn.sigmoid(ab))

    t = (i * tm + lax.broadcasted_iota(jnp.int32, (tm, 1), 0) + 1).astype(F32)
    for gi, win in enumerate(POOL_WINDOWS):
        lo = qkv_w + gi * POOL_GROUP_DIM
        seg = proj[:, lo:lo + POOL_GROUP_DIM]
        wsum = seg
        sh = 1
        while sh < win:
            wsum = wsum + pltpu.roll(wsum, sh, 0)
            sh *= 2
        inv_cnt = 1.0 / jnp.minimum(t, float(win))
        mixed = wsum[HALO:, :] * inv_cnt - seg[HALO:, :]
        y = _dot(mixed.astype(BF16), poolw_ref[gi])
        cs = slice(gi * POOL_GROUP_DIM, (gi + 1) * POOL_GROUP_DIM)
        yp_ref[0, :, cs] = (y * pscale_ref[:, cs]).astype(BF16)


def _mixer_proj(h, wcat, conv_w, alog_pad, dtb_pad, pool_w, pool_scale):
    bsz, seq, d = h.shape
    tm = min(TOKEN_TILE, seq)
    hb = tm // HALO
    row = lambda w: pl.BlockSpec((1, tm, w), lambda b, i: (b, i, 0))
    out_f = jax.ShapeDtypeStruct((bsz, seq, DN_WIDTH), F32)
    return pl.pallas_call(
        _mixer_proj_kernel,
        grid=(bsz, seq // tm),
        in_specs=[row(d),
                  pl.BlockSpec((1, HALO, d), lambda b, i: (b, jnp.maximum(i * hb - 1, 0), 0)),
                  _const_spec(wcat.shape), _const_spec(conv_w.shape), _const_spec(alog_pad.shape),
                  _const_spec(dtb_pad.shape), _const_spec(pool_w.shape), _const_spec(pool_scale.shape)],
        out_specs=[row(DN_WIDTH), row(DN_WIDTH), row(DN_WIDTH), row(AB_PAD), row(POOL_WIDTH)],
        out_shape=[out_f, out_f, out_f,
                   jax.ShapeDtypeStruct((bsz, seq, AB_PAD), F32),
                   jax.ShapeDtypeStruct((bsz, seq, POOL_WIDTH), BF16)],
        compiler_params=_params("parallel", "parallel"),
        name="mixer_proj",
    )(h, h, wcat, conv_w, alog_pad, dtb_pad, pool_w, pool_scale)


def _delta_kernel(q_ref, k_ref, v_ref, gb_ref, o_ref, state_ref):
    @pl.when(pl.program_id(0) == 0)
    def _():
        state_ref[...] = jnp.zeros_like(state_ref)

    c_len = DELTA_CHUNK
    bsz = q_ref.shape[0]
    n_chunks = q_ref.shape[1] // c_len
    ri = lax.broadcasted_iota(jnp.int32, (c_len, c_len), 0)
    ci = lax.broadcasted_iota(jnp.int32, (c_len, c_len), 1)
    incl = ri >= ci
    eye = ri == ci
    ltri = incl.astype(F32)
    level_masks = []
    s = 1
    while s < c_len:
        level_masks.append(((ri // (2 * s)) == (ci // (2 * s))) & ((ri // s) % 2 == 1) & ((ci // s) % 2 == 0))
        s *= 2
    lane = lax.broadcasted_iota(jnp.int32, (c_len, LANES), 1)

    chains = []
    for c in range(n_chunks):
        rows = slice(c * c_len, (c + 1) * c_len)
        for b in range(bsz):
            gbc = gb_ref[b, rows, :]
            g_only = jnp.where(lane < DN_HEADS, gbc, 0.0)
            gc = jnp.dot(ltri, g_only, precision=lax.Precision.HIGHEST,
                         preferred_element_type=F32)
            gct = gc.T
            e_gc = jnp.exp(gc)
            for hh in range(DN_HEADS):
                cs = slice(hh * DN_HEAD_DIM, (hh + 1) * DN_HEAD_DIM)
                chains.append(dict(
                    c=c, b=b, idx=b * DN_HEADS + hh, rows=rows, cs=cs,
                    beta=gbc[:, DN_HEADS + hh:DN_HEADS + hh + 1],
                    gcol=gc[:, hh:hh + 1],
                    grow=gct[hh:hh + 1, :],
                    glast=gct[hh:hh + 1, c_len - 1:c_len],
                    ecol=e_gc[:, hh:hh + 1]))

    for ch in chains:
        kh = k_ref[ch["b"], ch["rows"], ch["cs"]]
        ch["decay"] = jnp.exp(jnp.where(incl, ch["gcol"] - ch["grow"], -jnp.inf))
        ch["kb"] = kh * ch["beta"]
        ch["khb"] = kh.astype(BF16)
    for ch in chains:
        ch["a"] = jnp.where(eye, 0.0, _dot_nt(ch["kb"].astype(BF16), ch["khb"]) * ch["decay"])

    for ch in chains:
        ch["dinv"] = jnp.where(eye, 1.0, 0.0) - jnp.where(level_masks[0], ch["a"], 0.0)
    for m in level_masks[1:]:
        for ch in chains:
            ch["db"] = ch["dinv"].astype(BF16)
            ch["left"] = _dot(ch["db"], jnp.where(m, ch["a"], 0.0).astype(BF16))
        for ch in chains:
            ch["dinv"] = ch["dinv"] - _dot(ch["left"].astype(BF16), ch["db"])

    for ch in chains:
        b, rows, cs = ch["b"], ch["rows"], ch["cs"]
        t_low = jnp.where(eye, 0.0, ch["dinv"]).astype(BF16)
        rhs = jnp.concatenate([v_ref[b, rows, cs] * ch["beta"], ch["kb"] * ch["ecol"]], axis=1)
        sol = rhs + _dot(t_low, rhs.astype(BF16))
        ch["u"] = sol[:, :DN_HEAD_DIM]
        qh = q_ref[b, rows, cs]
        ch["wq"] = jnp.concatenate([sol[:, DN_HEAD_DIM:].astype(BF16), (qh * ch["ecol"]).astype(BF16)], axis=0)
        ch["attn"] = (_dot_nt(qh.astype(BF16), ch["khb"]) * ch["decay"]).astype(BF16)
        ch["kdt"] = (k_ref[b, rows, cs].T * jnp.exp(ch["glast"] - ch["grow"])).astype(BF16)

    for c in range(n_chunks):
        group = [ch for ch in chains if ch["c"] == c]
        for ch in group:
            ch["proj"] = _dot(ch["wq"], state_ref[ch["idx"]].astype(BF16))
        for ch in group:
            ch["v_new"] = (ch["u"] - ch["proj"][:c_len]).astype(BF16)
        for ch in group:
            o_ref[ch["b"], ch["rows"], ch["cs"]] = ch["proj"][c_len:] + _dot(ch["attn"], ch["v_new"])
        for ch in group:
            idx = ch["idx"]
            state_ref[idx] = state_ref[idx] * jnp.exp(ch["glast"]) + _dot(ch["kdt"], ch["v_new"])


def _delta_rule(q, k, v, gb):
    bsz, seq, width = q.shape
    rows = min(DELTA_STEP_CHUNKS * DELTA_CHUNK, seq)
    blk = lambda w: pl.BlockSpec((bsz, rows, w), lambda n: (0, n, 0))
    return pl.pallas_call(
        _delta_kernel,
        grid=(seq // rows,),
        in_specs=[blk(width), blk(width), blk(width), blk(AB_PAD)],
        out_specs=blk(width),
        out_shape=jax.ShapeDtypeStruct((bsz, seq, width), F32),
        scratch_shapes=[pltpu.VMEM((bsz * DN_HEADS, DN_HEAD_DIM, DN_HEAD_DIM), F32)],
        compiler_params=_params("arbitrary"),
        name="delta_rule",
    )(q, k, v, gb)


def _mixer_out_kernel(h_ref, o_ref, yp_ref, wz_ref, wgd_ref, wgp_ref, wdn_ref, wpool_ref, wmix_ref,
                      dnw_ref, g_ref, b_ref, out_ref):
    for rows in _row_groups(h_ref.shape[0]):
        h = h_ref[rows, :]
        hb = h.astype(BF16)
        z = _dot(hb, wz_ref[...])
        o = o_ref[rows, :]
        gated = []
        for hh in range(DN_HEADS):
            cs = slice(hh * DN_HEAD_DIM, (hh + 1) * DN_HEAD_DIM)
            oh = o[:, cs]
            oh = oh * lax.rsqrt(jnp.mean(oh * oh, axis=-1, keepdims=True) + RMS_EPS)
            gated.append((oh * dnw_ref[...] * _silu(z[:, cs])).astype(BF16))
        y_dn = _dot(jnp.concatenate(gated, axis=1), wdn_ref[...])
        y_pool = _dot(yp_ref[rows, :], wpool_ref[...])
        merged = (jax.nn.sigmoid(_dot(hb, wgd_ref[...])) * y_dn
                  + jax.nn.sigmoid(_dot(hb, wgp_ref[...])) * y_pool)
        mix = _dot(merged.astype(BF16), wmix_ref[...])
        out_ref[rows, :] = _layernorm(ALPHA * h + mix, g_ref[...], b_ref[...])


def _mixer_out(h2d, o2d, yp2d, wz, wgd, wgp, wdn, wpool, wmix, dnw, g, b):
    t, d = h2d.shape
    tm = min(TOKEN_TILE, t)
    row = lambda w: pl.BlockSpec((tm, w), lambda i: (i, 0))
    consts = (wz, wgd, wgp, wdn, wpool, wmix, dnw, g, b)
    return pl.pallas_call(
        _mixer_out_kernel,
        grid=(t // tm,),
        in_specs=[row(d), row(DN_WIDTH), row(POOL_WIDTH)] + [_const_spec(c.shape) for c in consts],
        out_specs=row(d),
        out_shape=jax.ShapeDtypeStruct((t, d), F32),
        compiler_params=_params("parallel"),
        name="mixer_out",
    )(h2d, o2d, yp2d, *consts)


def _mem_kv_kernel(m_ref, g_ref, b_ref, wk_ref, wv_ref, k_ref, v_ref):
    m = _layernorm(m_ref[0], g_ref[...], b_ref[...]).astype(BF16)
    k_ref[0] = _dot(m, wk_ref[...]).astype(BF16)
    v_ref[0] = _dot(m, wv_ref[...]).astype(BF16)


def _mem_kv(mem, g, b, wk, wv):
    bsz, n_mem, d = mem.shape
    blk = pl.BlockSpec((1, n_mem, d), lambda i: (i, 0, 0))
    out = jax.ShapeDtypeStruct((bsz, n_mem, d), BF16)
    return pl.pallas_call(
        _mem_kv_kernel,
        grid=(bsz,),
        in_specs=[blk, _const_spec(g.shape), _const_spec(b.shape), _const_spec(wk.shape), _const_spec(wv.shape)],
        out_specs=[blk, blk],
        out_shape=[out, out],
        compiler_params=_params("parallel"),
        name="mem_kv",
    )(mem, g, b, wk, wv)


def _xattn_ln_kernel(x_ref, k_ref, v_ref, wq_ref, wo_ref, g_ref, b_ref, o_ref):
    tm, d = x_ref.shape[1], x_ref.shape[2]
    hd = d // XA_HEADS
    cols = [slice(hh * hd, (hh + 1) * hd) for hh in range(XA_HEADS)]
    groups = _row_groups(tm)
    xs, scores = [], []
    for rows in groups:
        x = x_ref[0, rows, :]
        q = _dot(x.astype(BF16), wq_ref[...])
        xs.append(x)
        scores.append([_dot_nt(q[:, cs].astype(BF16), k_ref[0, :, cs]) * (hd ** -0.5) for cs in cols])
    for rows, x, group_scores in zip(groups, xs, scores):
        heads = []
        for cs, s in zip(cols, group_scores):
            e = jnp.exp(s - jnp.max(s, axis=-1, keepdims=True))
            pr = e * (1.0 / jnp.sum(e, axis=-1, keepdims=True))
            heads.append(_dot(pr.astype(BF16), v_ref[0, :, cs]).astype(BF16))
        xa = _dot(jnp.concatenate(heads, axis=1), wo_ref[...])
        o_ref[0, rows, :] = _layernorm(ALPHA * x + xa, g_ref[...], b_ref[...])


def _xattn_ln(x, k, v, wq, wo, g, b):
    bsz, seq, d = x.shape
    n_mem = k.shape[1]
    tm = min(TOKEN_TILE, seq)
    row = pl.BlockSpec((1, tm, d), lambda bb, i: (bb, i, 0))
    kv = pl.BlockSpec((1, n_mem, d), lambda bb, i: (bb, 0, 0))
    return pl.pallas_call(
        _xattn_ln_kernel,
        grid=(bsz, seq // tm),
        in_specs=[row, kv, kv, _const_spec(wq.shape), _const_spec(wo.shape),
                  _const_spec(g.shape), _const_spec(b.shape)],
        out_specs=row,
        out_shape=jax.ShapeDtypeStruct((bsz, seq, d), F32),
        compiler_params=_params("parallel", "parallel"),
        name="xattn_ln",
    )(x, k, v, wq, wo, g, b)


def _row(vec):
    return vec.reshape(1, -1).astype(F32)


def _pad_lanes(vec):
    return jnp.zeros((1, AB_PAD), F32).at[0, :vec.shape[0]].set(vec.astype(F32))


def kernel(x, mem, ffn1_w_gate, ffn1_w_up, ffn1_w_down, ln1_g, ln1_b, w_in, conv_w, a_log, dt_bias, dn_norm_w, w_dn_branch, pool_w, pool_scale, w_pool_branch, w_mix_out, ln2_g, ln2_b, mem_ln_g, mem_ln_b, xa_wq, xa_wk, xa_wv, xa_wo, ln3_g, ln3_b, ffn2_w_gate, ffn2_w_up, ffn2_w_down, ln4_g, ln4_b):
    bsz, seq, d = x.shape
    t = bsz * seq
    bf = lambda w: w.astype(BF16)

    for l in range(ffn1_w_gate.shape[0]):
        later = [ffn2_w_gate[l], ffn2_w_up[l], ffn2_w_down[l], w_dn_branch[l], w_pool_branch[l], w_mix_out[l],
                 xa_wq[l], xa_wk[l], xa_wv[l], xa_wo[l]]
        x2d, later = _ffn_ln(x.reshape(t, d), bf(ffn1_w_gate[l]), bf(ffn1_w_up[l]), bf(ffn1_w_down[l]),
                             _row(ln1_g[l]), _row(ln1_b[l]), riders=later, w_in=w_in[l])
        f2_gate, f2_up, f2_down, w_dn, w_pool, w_mix, wq, wk, wv, wo, wcat, w_z, w_gdn, w_gpool = later

        q, k, v, gb, ypool = _mixer_proj(x2d.reshape(bsz, seq, d), wcat, conv_w[l].astype(F32),
                                         _pad_lanes(a_log[l]), _pad_lanes(dt_bias[l]),
                                         bf(pool_w[l]), _row(pool_scale[l]))
        o = _delta_rule(q, k, v, gb)
        x2d = _mixer_out(x2d, o.reshape(t, DN_WIDTH), ypool.reshape(t, POOL_WIDTH),
                         w_z, w_gdn, w_gpool, w_dn, w_pool, w_mix,
                         _row(dn_norm_w[l]), _row(ln2_g[l]), _row(ln2_b[l]))

        mk, mv = _mem_kv(mem, _row(mem_ln_g[l]), _row(mem_ln_b[l]), wk, wv)
        x3 = _xattn_ln(x2d.reshape(bsz, seq, d), mk, mv, wq, wo, _row(ln3_g[l]), _row(ln3_b[l]))
        x, _ = _ffn_ln(x3.reshape(t, d), f2_gate, f2_up, f2_down, _row(ln4_g[l]), _row(ln4_b[l]))
        x = x.reshape(bsz, seq, d)
    return x
```

```python
import functools

import jax
import jax.numpy as jnp
from jax import lax
from jax.experimental import pallas as pl
from jax.experimental.pallas import tpu as pltpu

F32 = jnp.float32
BF16 = jnp.bfloat16

DN_HEADS = 4
DN_HEAD_DIM = 128
DN_WIDTH = DN_HEADS * DN_HEAD_DIM
CONV_K = 4
POOL_WINDOWS = (2, 4, 8, 16)
POOL_GROUP_DIM = 128
POOL_WIDTH = len(POOL_WINDOWS) * POOL_GROUP_DIM
XA_HEADS = 4
LN_EPS = 1e-5
RMS_EPS = 1e-6
L2_EPS = 1e-6
DEPTH = 1
ALPHA = (2.0 * DEPTH) ** 0.25

LANES = 128
BF16_ROWS = 16
HALO = 16
DELTA_CHUNK = 128
AB_PAD = LANES
VMEM_LIMIT = 60000 * 1024

TOKEN_TILE = 1024
DELTA_STEP_CHUNKS = 2
ROW_GROUP = 256
MXU_DIM = 256
FF_SLICE = 3 * MXU_DIM


def _dot(a, b):
    return jnp.dot(a, b, preferred_element_type=F32)


def _dot_nt(a, b):
    return lax.dot_general(a, b, (((1,), (1,)), ((), ())), preferred_element_type=F32)


def _layernorm(y, g, b):
    mu = jnp.mean(y, axis=-1, keepdims=True)
    yc = y - mu
    var = jnp.mean(yc * yc, axis=-1, keepdims=True)
    return yc * lax.rsqrt(var + LN_EPS) * g + b


def _silu(x):
    return x * jax.nn.sigmoid(x)


def _row_groups(n_rows):
    size = min(ROW_GROUP, n_rows)
    return [slice(lo, lo + size) for lo in range(0, n_rows, size)]


def _const_spec(shape):
    nd = len(shape)
    return pl.BlockSpec(shape, lambda *_: (0,) * nd, pipeline_mode=pl.Buffered(1))


def _params(*sem):
    return pltpu.CompilerParams(dimension_semantics=sem, vmem_limit_bytes=VMEM_LIMIT)


def _split_w_in(blk):
    o_qkv = 3 * DN_WIDTH
    o_z = o_qkv + DN_WIDTH
    d = (blk.shape[1] - o_z - 2 * DN_HEADS - POOL_WIDTH) // 2
    ab = blk[:, o_z:o_z + AB_PAD]
    lane = lax.broadcasted_iota(jnp.int32, ab.shape, 1)
    w_ab = jnp.where(lane < 2 * DN_HEADS, ab, 0.0)
    rest = blk[:, o_z + 2 * DN_HEADS:]
    wcat = jnp.concatenate([blk[:, :o_qkv], rest[:, :POOL_WIDTH], w_ab], axis=1)
    parts = (wcat, blk[:, o_qkv:o_z], rest[:, POOL_WIDTH:POOL_WIDTH + d], rest[:, POOL_WIDTH + d:])
    return [p.astype(BF16) for p in parts]


def _ffn_ln_kernel(n_riders, split_w_in, x_ref, wg_ref, wu_ref, wd_ref, g_ref, b_ref, *refs):
    n_in = n_riders + int(split_w_in)
    rider_in, o_ref, rider_out = refs[:n_in], refs[n_in], refs[n_in + 1:]
    for src, dst in zip(rider_in[:n_riders], rider_out[:n_riders]):
        dst[...] = src[...].astype(BF16)
    if split_w_in:
        for part, dst in zip(_split_w_in(rider_in[n_riders][...]), rider_out[n_riders:]):
            dst[...] = part

    ff = wg_ref.shape[1]
    slices = [slice(lo, min(lo + FF_SLICE, ff)) for lo in range(0, ff, FF_SLICE)]
    for rows in _row_groups(x_ref.shape[0]):
        x = x_ref[rows, :]
        xb = x.astype(BF16)
        gate_up = lambda cs: (_dot(xb, wg_ref[:, cs]), _dot(xb, wu_ref[:, cs]))
        y = None
        ahead = gate_up(slices[0])
        for n, cs in enumerate(slices):
            hg, hu = ahead
            if n + 1 < len(slices):
                ahead = gate_up(slices[n + 1])
            part = _dot((_silu(hg) * hu).astype(BF16), wd_ref[cs, :])
            y = part if y is None else y + part
        o_ref[rows, :] = _layernorm(ALPHA * x + 0.5 * y, g_ref[...], b_ref[...])


def _ffn_ln(x2d, wg, wu, wd, g, b, riders=(), w_in=None):
    t, d = x2d.shape
    ff = wg.shape[1]
    tm = min(TOKEN_TILE, t)
    steps = t // tm
    row = pl.BlockSpec((tm, d), lambda i: (i, 0))
    rider_specs = []
    for w in riders:
        rows = w.shape[0] // steps
        assert w.shape[0] == rows * steps and rows % BF16_ROWS == 0, w.shape
        rider_specs.append(pl.BlockSpec((rows, w.shape[1]), lambda i: (i, 0)))
    in_specs, out_specs = list(rider_specs), list(rider_specs)
    out_shape = [jax.ShapeDtypeStruct(w.shape, BF16) for w in riders]
    operands = list(riders)
    if w_in is not None:
        rows = w_in.shape[0] // steps
        assert w_in.shape[0] == rows * steps and rows % BF16_ROWS == 0, w_in.shape
        in_specs.append(pl.BlockSpec((rows, w_in.shape[1]), lambda i: (i, 0)))
        operands.append(w_in)
        for width in (3 * DN_WIDTH + POOL_WIDTH + AB_PAD, DN_WIDTH, d, d):
            out_specs.append(pl.BlockSpec((rows, width), lambda i: (i, 0)))
            out_shape.append(jax.ShapeDtypeStruct((w_in.shape[0], width), BF16))
    outs = pl.pallas_call(
        functools.partial(_ffn_ln_kernel, len(riders), w_in is not None),
        grid=(steps,),
        in_specs=[row, _const_spec((d, ff)), _const_spec((d, ff)), _const_spec((ff, d)),
                  _const_spec((1, d)), _const_spec((1, d))] + in_specs,
        out_specs=[row] + out_specs,
        out_shape=[jax.ShapeDtypeStruct((t, d), F32)] + out_shape,
        compiler_params=_params("parallel"),
        name="ffn_ln",
    )(x2d, wg, wu, wd, g, b, *operands)
    return outs[0], outs[1:]


def _project_rows(h, halo, first, t0, wcat_ref, convw_ref, alog_ref, dtb_ref, poolw_ref, pscale_ref):
    n = h.shape[0]
    qkv_w = 3 * DN_WIDTH
    halo = jnp.where(first, 0.0, halo)
    hcat = jnp.concatenate([halo, h], axis=0).astype(BF16)
    proj = _dot(hcat, wcat_ref[...])

    qkv_pre = proj[:, :qkv_w]
    conv = qkv_pre * convw_ref[CONV_K - 1:CONV_K, :]
    for s in range(1, CONV_K):
        conv = conv + pltpu.roll(qkv_pre, s, 0) * convw_ref[CONV_K - 1 - s:CONV_K - s, :]
    qkv = _silu(conv[HALO:, :])

    parts = []
    for part, scale in ((0, DN_HEAD_DIM ** -0.5), (1, None), (2, None)):
        heads = []
        for hh in range(DN_HEADS):
            lo = part * DN_WIDTH + hh * DN_HEAD_DIM
            seg = qkv[:, lo:lo + DN_HEAD_DIM]
            if part < 2:
                seg = seg * lax.rsqrt(jnp.sum(seg * seg, axis=-1, keepdims=True) + L2_EPS)
            if scale is not None:
                seg = seg * scale
            heads.append(seg)
        parts.append(jnp.concatenate(heads, axis=1))

    ab = proj[HALO:, qkv_w + POOL_WIDTH:]
    xa = ab + dtb_ref[...]
    softplus = jnp.maximum(xa, 0.0) + jnp.log1p(jnp.exp(-jnp.abs(xa)))
    gval = -jnp.exp(alog_ref[...]) * softplus
    lane = lax.broadcasted_iota(jnp.int32, ab.shape, 1)
    gb = jnp.where(lane < DN_HEADS, gval, jax.nn.sigmoid(ab))

    t = (t0 + lax.broadcasted_iota(jnp.int32, (n, 1), 0) + 1).astype(F32)
    pooled = []
    for gi, win in enumerate(POOL_WINDOWS):
        lo = qkv_w + gi * POOL_GROUP_DIM
        seg = proj[:, lo:lo + POOL_GROUP_DIM]
        wsum = seg
        sh = 1
        while sh < win:
            wsum = wsum + pltpu.roll(wsum, sh, 0)
            sh *= 2
        inv_cnt = 1.0 / jnp.minimum(t, float(win))
        mixed = wsum[HALO:, :] * inv_cnt - seg[HALO:, :]
        y = _dot(mixed.astype(BF16), poolw_ref[gi])
        cs = slice(gi * POOL_GROUP_DIM, (gi + 1) * POOL_GROUP_DIM)
        pooled.append((y * pscale_ref[:, cs]).astype(BF16))
    return parts[0], parts[1], parts[2], gb, jnp.concatenate(pooled, axis=1)


def _delta_stages(q_ref, k_ref, v_ref, gb_ref, o_ref, state_ref):
    c_len = DELTA_CHUNK
    bsz = q_ref.shape[0]
    n_chunks = q_ref.shape[1] // c_len
    ri = lax.broadcasted_iota(jnp.int32, (c_len, c_len), 0)
    ci = lax.broadcasted_iota(jnp.int32, (c_len, c_len), 1)
    incl = ri >= ci
    eye = ri == ci
    ltri = incl.astype(F32)
    level_masks = []
    s = 1
    while s < c_len:
        level_masks.append(((ri // (2 * s)) == (ci // (2 * s))) & ((ri // s) % 2 == 1) & ((ci // s) % 2 == 0))
        s *= 2
    lane = lax.broadcasted_iota(jnp.int32, (c_len, LANES), 1)

    chains = []
    for c in range(n_chunks):
        rows = slice(c * c_len, (c + 1) * c_len)
        for b in range(bsz):
            gbc = gb_ref[b, rows, :]
            g_only = jnp.where(lane < DN_HEADS, gbc, 0.0)
            gc = jnp.dot(ltri, g_only, precision=lax.Precision.HIGHEST,
                         preferred_element_type=F32)
            gct = gc.T
            e_gc = jnp.exp(gc)
            for hh in range(DN_HEADS):
                cs = slice(hh * DN_HEAD_DIM, (hh + 1) * DN_HEAD_DIM)
                chains.append(dict(
                    c=c, b=b, idx=b * DN_HEADS + hh, rows=rows, cs=cs,
                    q=q_ref[b, rows, cs], k=k_ref[b, rows, cs], v=v_ref[b, rows, cs],
                    beta=gbc[:, DN_HEADS + hh:DN_HEADS + hh + 1],
                    gcol=gc[:, hh:hh + 1],
                    grow=gct[hh:hh + 1, :],
                    glast=gct[hh:hh + 1, c_len - 1:c_len],
                    ecol=e_gc[:, hh:hh + 1]))

    for ch in chains:
        kh = ch["k"]
        ch["decay"] = jnp.exp(jnp.where(incl, ch["gcol"] - ch["grow"], -jnp.inf))
        ch["kb"] = kh * ch["beta"]
        ch["khb"] = kh.astype(BF16)
    for ch in chains:
        ch["a"] = jnp.where(eye, 0.0, _dot_nt(ch["kb"].astype(BF16), ch["khb"]) * ch["decay"])

    for ch in chains:
        ch["dinv"] = jnp.where(eye, 1.0, 0.0) - jnp.where(level_masks[0], ch["a"], 0.0)
    for m in level_masks[1:]:
        for ch in chains:
            ch["db"] = ch["dinv"].astype(BF16)
            ch["left"] = _dot(ch["db"], jnp.where(m, ch["a"], 0.0).astype(BF16))
        for ch in chains:
            ch["dinv"] = ch["dinv"] - _dot(ch["left"].astype(BF16), ch["db"])

    for ch in chains:
        t_low = jnp.where(eye, 0.0, ch["dinv"]).astype(BF16)
        rhs = jnp.concatenate([ch["v"] * ch["beta"], ch["kb"] * ch["ecol"]], axis=1)
        sol = rhs + _dot(t_low, rhs.astype(BF16))
        ch["u"] = sol[:, :DN_HEAD_DIM]
        qh = ch["q"]
        ch["wq"] = jnp.concatenate([sol[:, DN_HEAD_DIM:].astype(BF16), (qh * ch["ecol"]).astype(BF16)], axis=0)
        ch["attn"] = (_dot_nt(qh.astype(BF16), ch["khb"]) * ch["decay"]).astype(BF16)
        ch["kdt"] = (ch["k"].T * jnp.exp(ch["glast"] - ch["grow"])).astype(BF16)

    for c in range(n_chunks):
        group = [ch for ch in chains if ch["c"] == c]
        for ch in group:
            ch["proj"] = _dot(ch["wq"], state_ref[ch["idx"]].astype(BF16))
        for ch in group:
            ch["v_new"] = (ch["u"] - ch["proj"][:c_len]).astype(BF16)
        for ch in group:
            o_ref[ch["b"], ch["rows"], ch["cs"]] = ch["proj"][c_len:] + _dot(ch["attn"], ch["v_new"])
        for ch in group:
            idx = ch["idx"]
            state_ref[idx] = state_ref[idx] * jnp.exp(ch["glast"]) + _dot(ch["kdt"], ch["v_new"])


def _mixer_proj_kernel(h_ref, halo_ref, wcat_ref, convw_ref, alog_ref, dtb_ref, poolw_ref, pscale_ref,
                       q_ref, k_ref, v_ref, gb_ref, yp_ref):
    i = pl.program_id(1)
    outs = _project_rows(h_ref[0], halo_ref[0], i == 0, i * h_ref.shape[1],
                         wcat_ref, convw_ref, alog_ref, dtb_ref, poolw_ref, pscale_ref)
    for ref, val in zip((q_ref, k_ref, v_ref, gb_ref, yp_ref), outs):
        ref[0] = val


def _mixer_proj(h, wcat, conv_w, alog_pad, dtb_pad, pool_w, pool_scale):
    bsz, seq, d = h.shape
    tm = min(TOKEN_TILE, seq)
    hb = tm // HALO
    row = lambda w: pl.BlockSpec((1, tm, w), lambda b, i: (b, i, 0))
    out_f = jax.ShapeDtypeStruct((bsz, seq, DN_WIDTH), F32)
    consts = (wcat, conv_w, alog_pad, dtb_pad, pool_w, pool_scale)
    return pl.pallas_call(
        _mixer_proj_kernel,
        grid=(bsz, seq // tm),
        in_specs=[row(d), pl.BlockSpec((1, HALO, d), lambda b, i: (b, jnp.maximum(i * hb - 1, 0), 0))]
                 + [_const_spec(c.shape) for c in consts],
        out_specs=[row(DN_WIDTH), row(DN_WIDTH), row(DN_WIDTH), row(AB_PAD), row(POOL_WIDTH)],
        out_shape=[out_f, out_f, out_f,
                   jax.ShapeDtypeStruct((bsz, seq, AB_PAD), F32),
                   jax.ShapeDtypeStruct((bsz, seq, POOL_WIDTH), BF16)],
        compiler_params=_params("parallel", "parallel"),
        name="mixer_proj",
    )(h, h, *consts)


def _delta_kernel(q_ref, k_ref, v_ref, gb_ref, o_ref, state_ref):
    @pl.when(pl.program_id(0) == 0)
    def _():
        state_ref[...] = jnp.zeros_like(state_ref)

    _delta_stages(q_ref, k_ref, v_ref, gb_ref, o_ref, state_ref)


def _delta_rule(q, k, v, gb):
    bsz, seq, width = q.shape
    rows = min(DELTA_STEP_CHUNKS * DELTA_CHUNK, seq)
    blk = lambda w: pl.BlockSpec((bsz, rows, w), lambda n: (0, n, 0))
    return pl.pallas_call(
        _delta_kernel,
        grid=(seq // rows,),
        in_specs=[blk(width), blk(width), blk(width), blk(AB_PAD)],
        out_specs=blk(width),
        out_shape=jax.ShapeDtypeStruct((bsz, seq, width), F32),
        scratch_shapes=[pltpu.VMEM((bsz * DN_HEADS, DN_HEAD_DIM, DN_HEAD_DIM), F32)],
        compiler_params=_params("arbitrary"),
        name="delta_rule",
    )(q, k, v, gb)


def _mixer_out_kernel(h_ref, o_ref, yp_ref, wz_ref, wgd_ref, wgp_ref, wdn_ref, wpool_ref, wmix_ref,
                      dnw_ref, g_ref, b_ref, out_ref):
    def project(rows):
        hb = h_ref[rows, :].astype(BF16)
        return (_dot(hb, wz_ref[...]), _dot(hb, wgd_ref[...]), _dot(hb, wgp_ref[...]),
                _dot(yp_ref[rows, :], wpool_ref[...]))

    def finish(rows, z, gate_dn, gate_pool, y_pool):
        o = o_ref[rows, :]
        gated = []
        for hh in range(DN_HEADS):
            cs = slice(hh * DN_HEAD_DIM, (hh + 1) * DN_HEAD_DIM)
            oh = o[:, cs]
            oh = oh * lax.rsqrt(jnp.mean(oh * oh, axis=-1, keepdims=True) + RMS_EPS)
            gated.append((oh * dnw_ref[...] * _silu(z[:, cs])).astype(BF16))
        y_dn = _dot(jnp.concatenate(gated, axis=1), wdn_ref[...])
        merged = jax.nn.sigmoid(gate_dn) * y_dn + jax.nn.sigmoid(gate_pool) * y_pool
        mix = _dot(merged.astype(BF16), wmix_ref[...])
        out_ref[rows, :] = _layernorm(ALPHA * h_ref[rows, :] + mix, g_ref[...], b_ref[...])

    groups = _row_groups(h_ref.shape[0])
    ahead = project(groups[0])
    for n, rows in enumerate(groups):
        current = ahead
        if n + 1 < len(groups):
            ahead = project(groups[n + 1])
        finish(rows, *current)


def _mixer_out(h2d, o2d, yp2d, wz, wgd, wgp, wdn, wpool, wmix, dnw, g, b):
    t, d = h2d.shape
    tm = min(TOKEN_TILE, t)
    row = lambda w: pl.BlockSpec((tm, w), lambda i: (i, 0))
    consts = (wz, wgd, wgp, wdn, wpool, wmix, dnw, g, b)
    return pl.pallas_call(
        _mixer_out_kernel,
        grid=(t // tm,),
        in_specs=[row(d), row(DN_WIDTH), row(POOL_WIDTH)] + [_const_spec(c.shape) for c in consts],
        out_specs=row(d),
        out_shape=jax.ShapeDtypeStruct((t, d), F32),
        compiler_params=_params("parallel"),
        name="mixer_out",
    )(h2d, o2d, yp2d, *consts)


def _mem_kv_kernel(m_ref, g_ref, b_ref, wk_ref, wv_ref, k_ref, v_ref):
    m = _layernorm(m_ref[0], g_ref[...], b_ref[...]).astype(BF16)
    k_ref[0] = _dot(m, wk_ref[...]).astype(BF16)
    v_ref[0] = _dot(m, wv_ref[...]).astype(BF16)


def _mem_kv(mem, g, b, wk, wv):
    bsz, n_mem, d = mem.shape
    blk = pl.BlockSpec((1, n_mem, d), lambda i: (i, 0, 0))
    out = jax.ShapeDtypeStruct((bsz, n_mem, d), BF16)
    return pl.pallas_call(
        _mem_kv_kernel,
        grid=(bsz,),
        in_specs=[blk, _const_spec(g.shape), _const_spec(b.shape), _const_spec(wk.shape), _const_spec(wv.shape)],
        out_specs=[blk, blk],
        out_shape=[out, out],
        compiler_params=_params("parallel"),
        name="mem_kv",
    )(mem, g, b, wk, wv)


def _xattn_ln_kernel(x_ref, k_ref, v_ref, wq_ref, wo_ref, g_ref, b_ref, o_ref):
    tm, d = x_ref.shape[1], x_ref.shape[2]
    hd = d // XA_HEADS
    cols = [slice(hh * hd, (hh + 1) * hd) for hh in range(XA_HEADS)]
    groups = _row_groups(tm)
    xs, scores = [], []
    for rows in groups:
        x = x_ref[0, rows, :]
        q = _dot(x.astype(BF16), wq_ref[...])
        xs.append(x)
        scores.append([_dot_nt(q[:, cs].astype(BF16), k_ref[0, :, cs]) * (hd ** -0.5) for cs in cols])
    for rows, x, group_scores in zip(groups, xs, scores):
        heads = []
        for cs, s in zip(cols, group_scores):
            e = jnp.exp(s - jnp.max(s, axis=-1, keepdims=True))
            pr = e * (1.0 / jnp.sum(e, axis=-1, keepdims=True))
            heads.append(_dot(pr.astype(BF16), v_ref[0, :, cs]).astype(BF16))
        xa = _dot(jnp.concatenate(heads, axis=1), wo_ref[...])
        o_ref[0, rows, :] = _layernorm(ALPHA * x + xa, g_ref[...], b_ref[...])


def _xattn_ln(x, k, v, wq, wo, g, b):
    bsz, seq, d = x.shape
    n_mem = k.shape[1]
    tm = min(TOKEN_TILE, seq)
    row = pl.BlockSpec((1, tm, d), lambda bb, i: (bb, i, 0))
    kv = pl.BlockSpec((1, n_mem, d), lambda bb, i: (bb, 0, 0))
    return pl.pallas_call(
        _xattn_ln_kernel,
        grid=(bsz, seq // tm),
        in_specs=[row, kv, kv, _const_spec(wq.shape), _const_spec(wo.shape),
                  _const_spec(g.shape), _const_spec(b.shape)],
        out_specs=row,
        out_shape=jax.ShapeDtypeStruct((bsz, seq, d), F32),
        compiler_params=_params("parallel", "parallel"),
        name="xattn_ln",
    )(x, k, v, wq, wo, g, b)


def _row(vec):
    return vec.reshape(1, -1).astype(F32)


def _pad_lanes(vec):
    return jnp.zeros((1, AB_PAD), F32).at[0, :vec.shape[0]].set(vec.astype(F32))


def kernel(x, mem, ffn1_w_gate, ffn1_w_up, ffn1_w_down, ln1_g, ln1_b, w_in, conv_w, a_log, dt_bias, dn_norm_w, w_dn_branch, pool_w, pool_scale, w_pool_branch, w_mix_out, ln2_g, ln2_b, mem_ln_g, mem_ln_b, xa_wq, xa_wk, xa_wv, xa_wo, ln3_g, ln3_b, ffn2_w_gate, ffn2_w_up, ffn2_w_down, ln4_g, ln4_b):
    bsz, seq, d = x.shape
    t = bsz * seq
    bf = lambda w: w.astype(BF16)

    for l in range(ffn1_w_gate.shape[0]):
        later = [ffn2_w_gate[l], ffn2_w_up[l], ffn2_w_down[l], w_dn_branch[l], w_pool_branch[l], w_mix_out[l],
                 xa_wq[l], xa_wk[l], xa_wv[l], xa_wo[l]]
        x2d, later = _ffn_ln(x.reshape(t, d), bf(ffn1_w_gate[l]), bf(ffn1_w_up[l]), bf(ffn1_w_down[l]),
                             _row(ln1_g[l]), _row(ln1_b[l]), riders=later, w_in=w_in[l])
        f2_gate, f2_up, f2_down, w_dn, w_pool, w_mix, wq, wk, wv, wo, wcat, w_z, w_gdn, w_gpool = later

        q, k, v, gb, ypool = _mixer_proj(x2d.reshape(bsz, seq, d), wcat, conv_w[l].astype(F32),
                                         _pad_lanes(a_log[l]), _pad_lanes(dt_bias[l]),
                                         bf(pool_w[l]), _row(pool_scale[l]))
        o = _delta_rule(q, k, v, gb)
        x2d = _mixer_out(x2d, o.reshape(t, DN_WIDTH), ypool.reshape(t, POOL_WIDTH),
                         w_z, w_gdn, w_gpool, w_dn, w_pool, w_mix,
                         _row(dn_norm_w[l]), _row(ln2_g[l]), _row(ln2_b[l]))

        mk, mv = _mem_kv(mem, _row(mem_ln_g[l]), _row(mem_ln_b[l]), wk, wv)
        x3 = _xattn_ln(x2d.reshape(bsz, seq, d), mk, mv, wq, wo, _row(ln3_g[l]), _row(ln3_b[l]))
        x, _ = _ffn_ln(x3.reshape(t, d), f2_gate, f2_up, f2_down, _row(ln4_g[l]), _row(ln4_b[l]))
        x = x.reshape(bsz, seq, d)
    return x
```

```python
import functools

import jax
import jax.numpy as jnp
from jax import lax
from jax.experimental import pallas as pl
from jax.experimental.pallas import tpu as pltpu

F32 = jnp.float32
BF16 = jnp.bfloat16

DN_HEADS = 4
DN_HEAD_DIM = 128
DN_WIDTH = DN_HEADS * DN_HEAD_DIM
CONV_K = 4
POOL_WINDOWS = (2, 4, 8, 16)
POOL_GROUP_DIM = 128
POOL_WIDTH = len(POOL_WINDOWS) * POOL_GROUP_DIM
XA_HEADS = 4
LN_EPS = 1e-5
RMS_EPS = 1e-6
L2_EPS = 1e-6
DEPTH = 1
ALPHA = (2.0 * DEPTH) ** 0.25

LANES = 128
BF16_ROWS = 16
HALO = 16
DELTA_CHUNK = 128
AB_PAD = LANES
VMEM_LIMIT = 60000 * 1024

TOKEN_TILE = 1024
DELTA_STEP_CHUNKS = 2
ROW_GROUP = 256
MXU_DIM = 256
FF_SLICE = 3 * MXU_DIM


def _dot(a, b):
    return jnp.dot(a, b, preferred_element_type=F32)


def _dot_nt(a, b):
    return lax.dot_general(a, b, (((1,), (1,)), ((), ())), preferred_element_type=F32)


def _layernorm(y, g, b):
    mu = jnp.mean(y, axis=-1, keepdims=True)
    yc = y - mu
    var = jnp.mean(yc * yc, axis=-1, keepdims=True)
    return yc * lax.rsqrt(var + LN_EPS) * g + b


def _silu(x):
    return x * jax.nn.sigmoid(x)


def _row_groups(n_rows):
    size = min(ROW_GROUP, n_rows)
    return [slice(lo, lo + size) for lo in range(0, n_rows, size)]


def _const_spec(shape):
    nd = len(shape)
    return pl.BlockSpec(shape, lambda *_: (0,) * nd, pipeline_mode=pl.Buffered(1))


def _params(*sem):
    return pltpu.CompilerParams(dimension_semantics=sem, vmem_limit_bytes=VMEM_LIMIT)


def _split_w_in_t(blk_t):
    o_qkv = 3 * DN_WIDTH
    o_z = o_qkv + DN_WIDTH
    o_ab = o_z + 2 * DN_HEADS
    o_p = o_ab + POOL_WIDTH
    d = (blk_t.shape[0] - o_p) // 2
    kk = blk_t.shape[1]

    def rows_t(lo, hi):
        return jnp.concatenate([blk_t[r:r + LANES, :].T for r in range(lo, hi, LANES)], axis=1)

    ab = jnp.concatenate([blk_t[o_z:o_ab, :], jnp.zeros((AB_PAD - 2 * DN_HEADS, kk), F32)], axis=0).T
    wcat = jnp.concatenate([rows_t(0, o_qkv), rows_t(o_ab, o_p), ab], axis=1)
    parts = (wcat, rows_t(o_qkv, o_z), rows_t(o_p, o_p + d), rows_t(o_p + d, o_p + 2 * d))
    return [p.astype(BF16) for p in parts]


def _ffn_ln_kernel(n_riders, w_in_blocks, x_ref, wg_ref, wu_ref, wd_ref, g_ref, b_ref, *refs):
    n_in = n_riders + int(w_in_blocks > 0)
    rider_in, o_ref, rider_out = refs[:n_in], refs[n_in], refs[n_in + 1:]
    for src, dst in zip(rider_in[:n_riders], rider_out[:n_riders]):
        dst[...] = src[...].astype(BF16)
    if w_in_blocks:
        @pl.when(pl.program_id(0) < w_in_blocks)
        def _():
            for part, dst in zip(_split_w_in_t(rider_in[n_riders][...]), rider_out[n_riders:]):
                dst[...] = part

    ff = wg_ref.shape[1]
    slices = [slice(lo, min(lo + FF_SLICE, ff)) for lo in range(0, ff, FF_SLICE)]
    for rows in _row_groups(x_ref.shape[0]):
        x = x_ref[rows, :]
        xb = x.astype(BF16)
        gate_up = lambda cs: (_dot(xb, wg_ref[:, cs]), _dot(xb, wu_ref[:, cs]))
        y = None
        ahead = gate_up(slices[0])
        for n, cs in enumerate(slices):
            hg, hu = ahead
            if n + 1 < len(slices):
                ahead = gate_up(slices[n + 1])
            part = _dot((_silu(hg) * hu).astype(BF16), wd_ref[cs, :])
            y = part if y is None else y + part
        o_ref[rows, :] = _layernorm(ALPHA * x + 0.5 * y, g_ref[...], b_ref[...])


def _ffn_ln(x2d, wg, wu, wd, g, b, riders=(), w_in_t=None):
    t, d = x2d.shape
    ff = wg.shape[1]
    tm = min(TOKEN_TILE, t)
    steps = t // tm
    row = pl.BlockSpec((tm, d), lambda i: (i, 0))
    rider_specs = []
    for w in riders:
        rows = w.shape[0] // steps
        assert w.shape[0] == rows * steps and rows % BF16_ROWS == 0, w.shape
        rider_specs.append(pl.BlockSpec((rows, w.shape[1]), lambda i: (i, 0)))
    in_specs, out_specs = list(rider_specs), list(rider_specs)
    out_shape = [jax.ShapeDtypeStruct(w.shape, BF16) for w in riders]
    operands = list(riders)
    w_in_blocks = 0
    if w_in_t is not None:
        w_in_blocks = min(steps, d // LANES)
        kk = d // w_in_blocks
        slab = lambda i: jnp.minimum(i, w_in_blocks - 1)
        in_specs.append(pl.BlockSpec((w_in_t.shape[0], kk), lambda i: (0, slab(i))))
        operands.append(w_in_t)
        for width in (3 * DN_WIDTH + POOL_WIDTH + AB_PAD, DN_WIDTH, d, d):
            out_specs.append(pl.BlockSpec((kk, width), lambda i: (slab(i), 0)))
            out_shape.append(jax.ShapeDtypeStruct((d, width), BF16))
    outs = pl.pallas_call(
        functools.partial(_ffn_ln_kernel, len(riders), w_in_blocks),
        grid=(steps,),
        in_specs=[row, _const_spec((d, ff)), _const_spec((d, ff)), _const_spec((ff, d)),
                  _const_spec((1, d)), _const_spec((1, d))] + in_specs,
        out_specs=[row] + out_specs,
        out_shape=[jax.ShapeDtypeStruct((t, d), F32)] + out_shape,
        compiler_params=_params("parallel"),
        name="ffn_ln",
    )(x2d, wg, wu, wd, g, b, *operands)
    return outs[0], outs[1:]


def _project_rows(h, halo, first, t0, wcat_ref, convw_ref, alog_ref, dtb_ref, poolw_ref, pscale_ref):
    n = h.shape[0]
    qkv_w = 3 * DN_WIDTH
    halo = jnp.where(first, 0.0, halo)
    hcat = jnp.concatenate([halo, h], axis=0).astype(BF16)
    proj = _dot(hcat, wcat_ref[...])

    qkv_pre = proj[:, :qkv_w]
    conv = qkv_pre * convw_ref[CONV_K - 1:CONV_K, :]
    for s in range(1, CONV_K):
        conv = conv + pltpu.roll(qkv_pre, s, 0) * convw_ref[CONV_K - 1 - s:CONV_K - s, :]
    qkv = _silu(conv[HALO:, :])

    parts = []
    for part, scale in ((0, DN_HEAD_DIM ** -0.5), (1, None), (2, None)):
        heads = []
        for hh in range(DN_HEADS):
            lo = part * DN_WIDTH + hh * DN_HEAD_DIM
            seg = qkv[:, lo:lo + DN_HEAD_DIM]
            if part < 2:
                seg = seg * lax.rsqrt(jnp.sum(seg * seg, axis=-1, keepdims=True) + L2_EPS)
            if scale is not None:
                seg = seg * scale
            heads.append(seg)
        parts.append(jnp.concatenate(heads, axis=1))

    ab = proj[HALO:, qkv_w + POOL_WIDTH:]
    xa = ab + dtb_ref[...]
    softplus = jnp.maximum(xa, 0.0) + jnp.log1p(jnp.exp(-jnp.abs(xa)))
    gval = -jnp.exp(alog_ref[...]) * softplus
    lane = lax.broadcasted_iota(jnp.int32, ab.shape, 1)
    gb = jnp.where(lane < DN_HEADS, gval, jax.nn.sigmoid(ab))

    t = (t0 + lax.broadcasted_iota(jnp.int32, (n, 1), 0) + 1).astype(F32)
    pooled = []
    for gi, win in enumerate(POOL_WINDOWS):
        lo = qkv_w + gi * POOL_GROUP_DIM
        seg = proj[:, lo:lo + POOL_GROUP_DIM]
        wsum = seg
        sh = 1
        while sh < win:
            wsum = wsum + pltpu.roll(wsum, sh, 0)
            sh *= 2
        inv_cnt = 1.0 / jnp.minimum(t, float(win))
        mixed = wsum[HALO:, :] * inv_cnt - seg[HALO:, :]
        y = _dot(mixed.astype(BF16), poolw_ref[gi])
        cs = slice(gi * POOL_GROUP_DIM, (gi + 1) * POOL_GROUP_DIM)
        pooled.append((y * pscale_ref[:, cs]).astype(BF16))
    return parts[0], parts[1], parts[2], gb, jnp.concatenate(pooled, axis=1)


def _delta_stages(q_ref, k_ref, v_ref, gb_ref, o_ref, state_ref):
    c_len = DELTA_CHUNK
    bsz = q_ref.shape[0]
    n_chunks = q_ref.shape[1] // c_len
    ri = lax.broadcasted_iota(jnp.int32, (c_len, c_len), 0)
    ci = lax.broadcasted_iota(jnp.int32, (c_len, c_len), 1)
    incl = ri >= ci
    eye = ri == ci
    ltri = incl.astype(F32)
    level_masks = []
    s = 1
    while s < c_len:
        level_masks.append(((ri // (2 * s)) == (ci // (2 * s))) & ((ri // s) % 2 == 1) & ((ci // s) % 2 == 0))
        s *= 2
    lane = lax.broadcasted_iota(jnp.int32, (c_len, LANES), 1)

    chains = []
    for c in range(n_chunks):
        rows = slice(c * c_len, (c + 1) * c_len)
        for b in range(bsz):
            gbc = gb_ref[b, rows, :]
            g_only = jnp.where(lane < DN_HEADS, gbc, 0.0)
            gc = jnp.dot(ltri, g_only, precision=lax.Precision.HIGHEST,
                         preferred_element_type=F32)
            gct = gc.T
            e_gc = jnp.exp(gc)
            for hh in range(DN_HEADS):
                cs = slice(hh * DN_HEAD_DIM, (hh + 1) * DN_HEAD_DIM)
                chains.append(dict(
                    c=c, b=b, idx=b * DN_HEADS + hh, rows=rows, cs=cs,
                    q=q_ref[b, rows, cs], k=k_ref[b, rows, cs], v=v_ref[b, rows, cs],
                    beta=gbc[:, DN_HEADS + hh:DN_HEADS + hh + 1],
                    gcol=gc[:, hh:hh + 1],
                    grow=gct[hh:hh + 1, :],
                    glast=gct[hh:hh + 1, c_len - 1:c_len],
                    ecol=e_gc[:, hh:hh + 1]))

    for ch in chains:
        kh = ch["k"]
        ch["decay"] = jnp.exp(jnp.where(incl, ch["gcol"] - ch["grow"], -jnp.inf))
        ch["kb"] = kh * ch["beta"]
        ch["khb"] = kh.astype(BF16)
    for ch in chains:
        ch["a"] = jnp.where(eye, 0.0, _dot_nt(ch["kb"].astype(BF16), ch["khb"]) * ch["decay"])

    for ch in chains:
        ch["dinv"] = jnp.where(eye, 1.0, 0.0) - jnp.where(level_masks[0], ch["a"], 0.0)
    for m in level_masks[1:]:
        for ch in chains:
            ch["db"] = ch["dinv"].astype(BF16)
            ch["left"] = _dot(ch["db"], jnp.where(m, ch["a"], 0.0).astype(BF16))
        for ch in chains:
            ch["dinv"] = ch["dinv"] - _dot(ch["left"].astype(BF16), ch["db"])

    for ch in chains:
        t_low = jnp.where(eye, 0.0, ch["dinv"]).astype(BF16)
        rhs = jnp.concatenate([ch["v"] * ch["beta"], ch["kb"] * ch["ecol"]], axis=1)
        sol = rhs + _dot(t_low, rhs.astype(BF16))
        ch["u"] = sol[:, :DN_HEAD_DIM]
        qh = ch["q"]
        ch["wq"] = jnp.concatenate([sol[:, DN_HEAD_DIM:].astype(BF16), (qh * ch["ecol"]).astype(BF16)], axis=0)
        ch["attn"] = (_dot_nt(qh.astype(BF16), ch["khb"]) * ch["decay"]).astype(BF16)
        ch["kdt"] = (ch["k"].T * jnp.exp(ch["glast"] - ch["grow"])).astype(BF16)

    for c in range(n_chunks):
        group = [ch for ch in chains if ch["c"] == c]
        for ch in group:
            ch["proj"] = _dot(ch["wq"], state_ref[ch["idx"]].astype(BF16))
        for ch in group:
            ch["v_new"] = (ch["u"] - ch["proj"][:c_len]).astype(BF16)
        for ch in group:
            o_ref[ch["b"], ch["rows"], ch["cs"]] = ch["proj"][c_len:] + _dot(ch["attn"], ch["v_new"])
        for ch in group:
            idx = ch["idx"]
            state_ref[idx] = state_ref[idx] * jnp.exp(ch["glast"]) + _dot(ch["kdt"], ch["v_new"])


def _mixer_proj_kernel(h_ref, halo_ref, wcat_ref, convw_ref, alog_ref, dtb_ref, poolw_ref, pscale_ref,
                       q_ref, k_ref, v_ref, gb_ref, yp_ref):
    i = pl.program_id(1)
    outs = _project_rows(h_ref[0], halo_ref[0], i == 0, i * h_ref.shape[1],
                         wcat_ref, convw_ref, alog_ref, dtb_ref, poolw_ref, pscale_ref)
    for ref, val in zip((q_ref, k_ref, v_ref, gb_ref, yp_ref), outs):
        ref[0] = val


def _mixer_proj(h, wcat, conv_w, alog_pad, dtb_pad, pool_w, pool_scale):
    bsz, seq, d = h.shape
    tm = min(TOKEN_TILE, seq)
    hb = tm // HALO
    row = lambda w: pl.BlockSpec((1, tm, w), lambda b, i: (b, i, 0))
    out_f = jax.ShapeDtypeStruct((bsz, seq, DN_WIDTH), F32)
    consts = (wcat, conv_w, alog_pad, dtb_pad, pool_w, pool_scale)
    return pl.pallas_call(
        _mixer_proj_kernel,
        grid=(bsz, seq // tm),
        in_specs=[row(d), pl.BlockSpec((1, HALO, d), lambda b, i: (b, jnp.maximum(i * hb - 1, 0), 0))]
                 + [_const_spec(c.shape) for c in consts],
        out_specs=[row(DN_WIDTH), row(DN_WIDTH), row(DN_WIDTH), row(AB_PAD), row(POOL_WIDTH)],
        out_shape=[out_f, out_f, out_f,
                   jax.ShapeDtypeStruct((bsz, seq, AB_PAD), F32),
                   jax.ShapeDtypeStruct((bsz, seq, POOL_WIDTH), BF16)],
        compiler_params=_params("parallel", "parallel"),
        name="mixer_proj",
    )(h, h, *consts)


def _delta_kernel(q_ref, k_ref, v_ref, gb_ref, o_ref, state_ref):
    @pl.when(pl.program_id(0) == 0)
    def _():
        state_ref[...] = jnp.zeros_like(state_ref)

    _delta_stages(q_ref, k_ref, v_ref, gb_ref, o_ref, state_ref)


def _delta_rule(q, k, v, gb):
    bsz, seq, width = q.shape
    rows = min(DELTA_STEP_CHUNKS * DELTA_CHUNK, seq)
    blk = lambda w: pl.BlockSpec((bsz, rows, w), lambda n: (0, n, 0))
    return pl.pallas_call(
        _delta_kernel,
        grid=(seq // rows,),
        in_specs=[blk(width), blk(width), blk(width), blk(AB_PAD)],
        out_specs=blk(width),
        out_shape=jax.ShapeDtypeStruct((bsz, seq, width), F32),
        scratch_shapes=[pltpu.VMEM((bsz * DN_HEADS, DN_HEAD_DIM, DN_HEAD_DIM), F32)],
        compiler_params=_params("arbitrary"),
        name="delta_rule",
    )(q, k, v, gb)


def _mixer_out_kernel(h_ref, o_ref, yp_ref, wz_ref, wgd_ref, wgp_ref, wdn_ref, wpool_ref, wmix_ref,
                      dnw_ref, g_ref, b_ref, out_ref):
    def project(rows):
        hb = h_ref[rows, :].astype(BF16)
        return (_dot(hb, wz_ref[...]), _dot(hb, wgd_ref[...]), _dot(hb, wgp_ref[...]),
                _dot(yp_ref[rows, :], wpool_ref[...]))

    def finish(rows, z, gate_dn, gate_pool, y_pool):
        o = o_ref[rows, :]
        gated = []
        for hh in range(DN_HEADS):
            cs = slice(hh * DN_HEAD_DIM, (hh + 1) * DN_HEAD_DIM)
            oh = o[:, cs]
            oh = oh * lax.rsqrt(jnp.mean(oh * oh, axis=-1, keepdims=True) + RMS_EPS)
            gated.append((oh * dnw_ref[...] * _silu(z[:, cs])).astype(BF16))
        y_dn = _dot(jnp.concatenate(gated, axis=1), wdn_ref[...])
        merged = jax.nn.sigmoid(gate_dn) * y_dn + jax.nn.sigmoid(gate_pool) * y_pool
        mix = _dot(merged.astype(BF16), wmix_ref[...])
        out_ref[rows, :] = _layernorm(ALPHA * h_ref[rows, :] + mix, g_ref[...], b_ref[...])

    groups = _row_groups(h_ref.shape[0])
    ahead = project(groups[0])
    for n, rows in enumerate(groups):
        current = ahead
        if n + 1 < len(groups):
            ahead = project(groups[n + 1])
        finish(rows, *current)


def _mixer_out(h2d, o2d, yp2d, wz, wgd, wgp, wdn, wpool, wmix, dnw, g, b):
    t, d = h2d.shape
    tm = min(TOKEN_TILE, t)
    row = lambda w: pl.BlockSpec((tm, w), lambda i: (i, 0))
    consts = (wz, wgd, wgp, wdn, wpool, wmix, dnw, g, b)
    return pl.pallas_call(
        _mixer_out_kernel,
        grid=(t // tm,),
        in_specs=[row(d), row(DN_WIDTH), row(POOL_WIDTH)] + [_const_spec(c.shape) for c in consts],
        out_specs=row(d),
        out_shape=jax.ShapeDtypeStruct((t, d), F32),
        compiler_params=_params("parallel"),
        name="mixer_out",
    )(h2d, o2d, yp2d, *consts)


def _mem_kv_kernel(m_ref, g_ref, b_ref, wk_ref, wv_ref, k_ref, v_ref):
    m = _layernorm(m_ref[0], g_ref[...], b_ref[...]).astype(BF16)
    k_ref[0] = _dot(m, wk_ref[...]).astype(BF16)
    v_ref[0] = _dot(m, wv_ref[...]).astype(BF16)


def _mem_kv(mem, g, b, wk, wv):
    bsz, n_mem, d = mem.shape
    blk = pl.BlockSpec((1, n_mem, d), lambda i: (i, 0, 0))
    out = jax.ShapeDtypeStruct((bsz, n_mem, d), BF16)
    return pl.pallas_call(
        _mem_kv_kernel,
        grid=(bsz,),
        in_specs=[blk, _const_spec(g.shape), _const_spec(b.shape), _const_spec(wk.shape), _const_spec(wv.shape)],
        out_specs=[blk, blk],
        out_shape=[out, out],
        compiler_params=_params("parallel"),
        name="mem_kv",
    )(mem, g, b, wk, wv)


def _xattn_ln_kernel(x_ref, k_ref, v_ref, wq_ref, wo_ref, g_ref, b_ref, o_ref):
    tm, d = x_ref.shape[1], x_ref.shape[2]
    hd = d // XA_HEADS
    cols = [slice(hh * hd, (hh + 1) * hd) for hh in range(XA_HEADS)]
    groups = _row_groups(tm)
    xs, scores = [], []
    for rows in groups:
        x = x_ref[0, rows, :]
        q = _dot(x.astype(BF16), wq_ref[...])
        xs.append(x)
        scores.append([_dot_nt(q[:, cs].astype(BF16), k_ref[0, :, cs]) * (hd ** -0.5) for cs in cols])
    for rows, x, group_scores in zip(groups, xs, scores):
        heads = []
        for cs, s in zip(cols, group_scores):
            e = jnp.exp(s - jnp.max(s, axis=-1, keepdims=True))
            pr = e * (1.0 / jnp.sum(e, axis=-1, keepdims=True))
            heads.append(_dot(pr.astype(BF16), v_ref[0, :, cs]).astype(BF16))
        xa = _dot(jnp.concatenate(heads, axis=1), wo_ref[...])
        o_ref[0, rows, :] = _layernorm(ALPHA * x + xa, g_ref[...], b_ref[...])


def _xattn_ln(x, k, v, wq, wo, g, b):
    bsz, seq, d = x.shape
    n_mem = k.shape[1]
    tm = min(TOKEN_TILE, seq)
    row = pl.BlockSpec((1, tm, d), lambda bb, i: (bb, i, 0))
    kv = pl.BlockSpec((1, n_mem, d), lambda bb, i: (bb, 0, 0))
    return pl.pallas_call(
        _xattn_ln_kernel,
        grid=(bsz, seq // tm),
        in_specs=[row, kv, kv, _const_spec(wq.shape), _const_spec(wo.shape),
                  _const_spec(g.shape), _const_spec(b.shape)],
        out_specs=row,
        out_shape=jax.ShapeDtypeStruct((bsz, seq, d), F32),
        compiler_params=_params("parallel", "parallel"),
        name="xattn_ln",
    )(x, k, v, wq, wo, g, b)


def _row(vec):
    return vec.reshape(1, -1).astype(F32)


def _pad_lanes(vec):
    return jnp.zeros((1, AB_PAD), F32).at[0, :vec.shape[0]].set(vec.astype(F32))


def kernel(x, mem, ffn1_w_gate, ffn1_w_up, ffn1_w_down, ln1_g, ln1_b, w_in, conv_w, a_log, dt_bias, dn_norm_w, w_dn_branch, pool_w, pool_scale, w_pool_branch, w_mix_out, ln2_g, ln2_b, mem_ln_g, mem_ln_b, xa_wq, xa_wk, xa_wv, xa_wo, ln3_g, ln3_b, ffn2_w_gate, ffn2_w_up, ffn2_w_down, ln4_g, ln4_b):
    bsz, seq, d = x.shape
    t = bsz * seq
    bf = lambda w: w.astype(BF16)

    for l in range(ffn1_w_gate.shape[0]):
        later = [ffn2_w_gate[l], ffn2_w_up[l], ffn2_w_down[l], w_dn_branch[l], w_pool_branch[l], w_mix_out[l],
                 xa_wq[l], xa_wk[l], xa_wv[l], xa_wo[l]]
        x2d, later = _ffn_ln(x.reshape(t, d), bf(ffn1_w_gate[l]), bf(ffn1_w_up[l]), bf(ffn1_w_down[l]),
                             _row(ln1_g[l]), _row(ln1_b[l]), riders=later, w_in_t=jnp.swapaxes(w_in[l], 0, 1))
        f2_gate, f2_up, f2_down, w_dn, w_pool, w_mix, wq, wk, wv, wo, wcat, w_z, w_gdn, w_gpool = later

        q, k, v, gb, ypool = _mixer_proj(x2d.reshape(bsz, seq, d), wcat, conv_w[l].astype(F32),
                                         _pad_lanes(a_log[l]), _pad_lanes(dt_bias[l]),
                                         bf(pool_w[l]), _row(pool_scale[l]))
        o = _delta_rule(q, k, v, gb)
        x2d = _mixer_out(x2d, o.reshape(t, DN_WIDTH), ypool.reshape(t, POOL_WIDTH),
                         w_z, w_gdn, w_gpool, w_dn, w_pool, w_mix,
                         _row(dn_norm_w[l]), _row(ln2_g[l]), _row(ln2_b[l]))

        mk, mv = _mem_kv(mem, _row(mem_ln_g[l]), _row(mem_ln_b[l]), wk, wv)
        x3 = _xattn_ln(x2d.reshape(bsz, seq, d), mk, mv, wq, wo, _row(ln3_g[l]), _row(ln3_b[l]))
        x, _ = _ffn_ln(x3.reshape(t, d), f2_gate, f2_up, f2_down, _row(ln4_g[l]), _row(ln4_b[l]))
        x = x.reshape(bsz, seq, d)
    return x
```

```python
import functools

import jax
import jax.numpy as jnp
from jax import lax
from jax.experimental import pallas as pl
from jax.experimental.pallas import tpu as pltpu

F32 = jnp.float32
BF16 = jnp.bfloat16

DN_HEADS = 4
DN_HEAD_DIM = 128
DN_WIDTH = DN_HEADS * DN_HEAD_DIM
CONV_K = 4
POOL_WINDOWS = (2, 4, 8, 16)
POOL_GROUP_DIM = 128
POOL_WIDTH = len(POOL_WINDOWS) * POOL_GROUP_DIM
XA_HEADS = 4
LN_EPS = 1e-5
RMS_EPS = 1e-6
L2_EPS = 1e-6
DEPTH = 1
ALPHA = (2.0 * DEPTH) ** 0.25

LANES = 128
BF16_ROWS = 16
HALO = 16
DELTA_CHUNK = 128
AB_PAD = LANES
VMEM_LIMIT = 60000 * 1024

TOKEN_TILE = 1024
DELTA_STEP_CHUNKS = 2
ROW_GROUP = 256
MXU_DIM = 256
FF_SLICE = 3 * MXU_DIM


def _dot(a, b):
    return jnp.dot(a, b, preferred_element_type=F32)


def _dot_nt(a, b):
    return lax.dot_general(a, b, (((1,), (1,)), ((), ())), preferred_element_type=F32)


def _layernorm(y, g, b):
    mu = jnp.mean(y, axis=-1, keepdims=True)
    yc = y - mu
    var = jnp.mean(yc * yc, axis=-1, keepdims=True)
    return yc * lax.rsqrt(var + LN_EPS) * g + b


def _silu(x):
    return x * jax.nn.sigmoid(x)


def _row_groups(n_rows):
    size = min(ROW_GROUP, n_rows)
    return [slice(lo, lo + size) for lo in range(0, n_rows, size)]


def _const_spec(shape):
    nd = len(shape)
    return pl.BlockSpec(shape, lambda *_: (0,) * nd, pipeline_mode=pl.Buffered(1))


def _params(*sem):
    return pltpu.CompilerParams(dimension_semantics=sem, vmem_limit_bytes=VMEM_LIMIT)


def _split_w_in_t(blk_t):
    o_qkv = 3 * DN_WIDTH
    o_z = o_qkv + DN_WIDTH
    o_ab = o_z + 2 * DN_HEADS
    o_p = o_ab + POOL_WIDTH
    d = (blk_t.shape[0] - o_p) // 2
    kk = blk_t.shape[1]

    def rows_t(lo, hi):
        return jnp.concatenate([blk_t[r:r + LANES, :].T for r in range(lo, hi, LANES)], axis=1)

    ab = jnp.concatenate([blk_t[o_z:o_ab, :], jnp.zeros((AB_PAD - 2 * DN_HEADS, kk), F32)], axis=0).T
    wcat = jnp.concatenate([rows_t(0, o_qkv), rows_t(o_ab, o_p), ab], axis=1)
    parts = (wcat, rows_t(o_qkv, o_z), rows_t(o_p, o_p + d), rows_t(o_p + d, o_p + 2 * d))
    return [p.astype(BF16) for p in parts]


def _ffn_ln_kernel(n_riders, w_in_blocks, x_ref, wg_ref, wu_ref, wd_ref, g_ref, b_ref, *refs):
    n_in = n_riders + int(w_in_blocks > 0)
    rider_in, o_ref, rider_out = refs[:n_in], refs[n_in], refs[n_in + 1:]
    for src, dst in zip(rider_in[:n_riders], rider_out[:n_riders]):
        dst[...] = src[...].astype(BF16)
    if w_in_blocks:
        @pl.when(pl.program_id(0) < w_in_blocks)
        def _():
            for part, dst in zip(_split_w_in_t(rider_in[n_riders][...]), rider_out[n_riders:]):
                dst[...] = part

    ff = wg_ref.shape[1]
    slices = [slice(lo, min(lo + FF_SLICE, ff)) for lo in range(0, ff, FF_SLICE)]
    for rows in _row_groups(x_ref.shape[0]):
        x = x_ref[rows, :]
        xb = x.astype(BF16)
        gate_up = lambda cs: (_dot(xb, wg_ref[:, cs]), _dot(xb, wu_ref[:, cs]))
        y = None
        ahead = gate_up(slices[0])
        for n, cs in enumerate(slices):
            hg, hu = ahead
            if n + 1 < len(slices):
                ahead = gate_up(slices[n + 1])
            part = _dot((_silu(hg) * hu).astype(BF16), wd_ref[cs, :])
            y = part if y is None else y + part
        o_ref[rows, :] = _layernorm(ALPHA * x + 0.5 * y, g_ref[...], b_ref[...])


def _ffn_ln(x2d, wg, wu, wd, g, b, riders=(), w_in_t=None):
    t, d = x2d.shape
    ff = wg.shape[1]
    tm = min(TOKEN_TILE, t)
    steps = t // tm
    row = pl.BlockSpec((tm, d), lambda i: (i, 0))
    rider_specs = []
    for w in riders:
        rows = w.shape[0] // steps
        assert w.shape[0] == rows * steps and rows % BF16_ROWS == 0, w.shape
        rider_specs.append(pl.BlockSpec((rows, w.shape[1]), lambda i: (i, 0)))
    in_specs, out_specs = list(rider_specs), list(rider_specs)
    out_shape = [jax.ShapeDtypeStruct(w.shape, BF16) for w in riders]
    operands = list(riders)
    w_in_blocks = 0
    if w_in_t is not None:
        w_in_blocks = min(steps, d // LANES)
        kk = d // w_in_blocks
        slab = lambda i: jnp.minimum(i, w_in_blocks - 1)
        in_specs.append(pl.BlockSpec((w_in_t.shape[0], kk), lambda i: (0, slab(i))))
        operands.append(w_in_t)
        for width in (3 * DN_WIDTH + POOL_WIDTH + AB_PAD, DN_WIDTH, d, d):
            out_specs.append(pl.BlockSpec((kk, width), lambda i: (slab(i), 0)))
            out_shape.append(jax.ShapeDtypeStruct((d, width), BF16))
    outs = pl.pallas_call(
        functools.partial(_ffn_ln_kernel, len(riders), w_in_blocks),
        grid=(steps,),
        in_specs=[row, _const_spec((d, ff)), _const_spec((d, ff)), _const_spec((ff, d)),
                  _const_spec((1, d)), _const_spec((1, d))] + in_specs,
        out_specs=[row] + out_specs,
        out_shape=[jax.ShapeDtypeStruct((t, d), F32)] + out_shape,
        compiler_params=_params("parallel"),
        name="ffn_ln",
    )(x2d, wg, wu, wd, g, b, *operands)
    return outs[0], outs[1:]


def _project_rows(h, halo, first, t0, wcat_ref, convw_ref, alog_ref, dtb_ref, poolw_ref, pscale_ref):
    n = h.shape[0]
    qkv_w = 3 * DN_WIDTH
    halo = jnp.where(first, 0.0, halo)
    hcat = jnp.concatenate([halo, h], axis=0).astype(BF16)
    proj = _dot(hcat, wcat_ref[...])

    qkv_pre = proj[:, :qkv_w]
    conv = qkv_pre * convw_ref[CONV_K - 1:CONV_K, :]
    for s in range(1, CONV_K):
        conv = conv + pltpu.roll(qkv_pre, s, 0) * convw_ref[CONV_K - 1 - s:CONV_K - s, :]
    qkv = _silu(conv[HALO:, :])

    parts = []
    for part, scale in ((0, DN_HEAD_DIM ** -0.5), (1, None), (2, None)):
        heads = []
        for hh in range(DN_HEADS):
            lo = part * DN_WIDTH + hh * DN_HEAD_DIM
            seg = qkv[:, lo:lo + DN_HEAD_DIM]
            if part < 2:
                inv = lax.rsqrt(jnp.sum(seg * seg, axis=-1, keepdims=True) + L2_EPS)
                seg = seg * (inv if scale is None else inv * scale)
            heads.append(seg)
        parts.append(jnp.concatenate(heads, axis=1))

    ab = proj[HALO:, qkv_w + POOL_WIDTH:]
    xa = ab + dtb_ref[...]
    softplus = jnp.maximum(xa, 0.0) + jnp.log1p(jnp.exp(-jnp.abs(xa)))
    gval = -jnp.exp(alog_ref[...]) * softplus
    lane = lax.broadcasted_iota(jnp.int32, ab.shape, 1)
    gb = jnp.where(lane < DN_HEADS, gval, jax.nn.sigmoid(ab))

    t = (t0 + lax.broadcasted_iota(jnp.int32, (n, 1), 0) + 1).astype(F32)
    pooled = []
    for gi, win in enumerate(POOL_WINDOWS):
        lo = qkv_w + gi * POOL_GROUP_DIM
        seg = proj[:, lo:lo + POOL_GROUP_DIM]
        wsum = seg
        sh = 1
        while sh < win:
            wsum = wsum + pltpu.roll(wsum, sh, 0)
            sh *= 2
        inv_cnt = 1.0 / jnp.minimum(t, float(win))
        mixed = wsum[HALO:, :] * inv_cnt - seg[HALO:, :]
        y = _dot(mixed.astype(BF16), poolw_ref[gi])
        cs = slice(gi * POOL_GROUP_DIM, (gi + 1) * POOL_GROUP_DIM)
        pooled.append((y * pscale_ref[:, cs]).astype(BF16))
    return parts[0], parts[1], parts[2], gb, jnp.concatenate(pooled, axis=1)


def _delta_stages(q_ref, k_ref, v_ref, gb_ref, o_ref, state_ref):
    c_len = DELTA_CHUNK
    bsz = q_ref.shape[0]
    n_chunks = q_ref.shape[1] // c_len
    ri = lax.broadcasted_iota(jnp.int32, (c_len, c_len), 0)
    ci = lax.broadcasted_iota(jnp.int32, (c_len, c_len), 1)
    incl = ri >= ci
    eye = ri == ci
    level_masks = []
    s = 1
    while s < c_len:
        level_masks.append(((ri // (2 * s)) == (ci // (2 * s))) & ((ri // s) % 2 == 1) & ((ci // s) % 2 == 0))
        s *= 2
    lane = lax.broadcasted_iota(jnp.int32, (c_len, LANES), 1)

    chains = []
    for c in range(n_chunks):
        rows = slice(c * c_len, (c + 1) * c_len)
        for b in range(bsz):
            gbc = gb_ref[b, rows, :]
            g_only = jnp.where(lane < DN_HEADS, gbc, 0.0)
            gc = jnp.dot(incl.astype(F32), g_only, precision=lax.Precision.HIGHEST,
                         preferred_element_type=F32)
            gct = gc.T
            e_gc = jnp.exp(gc)
            for hh in range(DN_HEADS):
                cs = slice(hh * DN_HEAD_DIM, (hh + 1) * DN_HEAD_DIM)
                chains.append(dict(
                    c=c, b=b, idx=b * DN_HEADS + hh, rows=rows, cs=cs,
                    q=q_ref[b, rows, cs], k=k_ref[b, rows, cs], v=v_ref[b, rows, cs],
                    beta=gbc[:, DN_HEADS + hh:DN_HEADS + hh + 1],
                    gcol=gc[:, hh:hh + 1],
                    grow=gct[hh:hh + 1, :],
                    glast=gct[hh:hh + 1, c_len - 1:c_len],
                    ecol=e_gc[:, hh:hh + 1]))

    for ch in chains:
        kh = ch["k"]
        ch["decay"] = jnp.exp(jnp.where(incl, ch["gcol"] - ch["grow"], -jnp.inf))
        ch["kb"] = kh * ch["beta"]
        ch["khb"] = kh.astype(BF16)
    for ch in chains:
        both = _dot_nt(jnp.concatenate([ch["kb"].astype(BF16), ch["q"].astype(BF16)], axis=0), ch["khb"])
        ch["a"] = jnp.where(eye, 0.0, both[:c_len] * ch["decay"])
        ch["attn"] = (both[c_len:] * ch["decay"]).astype(BF16)

    for ch in chains:
        ch["dinv"] = jnp.where(eye, 1.0, 0.0) - jnp.where(level_masks[0], ch["a"], 0.0)
    for m in level_masks[1:]:
        for ch in chains:
            ch["db"] = ch["dinv"].astype(BF16)
            ch["left"] = _dot(ch["db"], jnp.where(m, ch["a"], 0.0).astype(BF16))
        for ch in chains:
            ch["dinv"] = ch["dinv"] - _dot(ch["left"].astype(BF16), ch["db"])

    for ch in chains:
        t_low = jnp.where(eye, 0.0, ch["dinv"]).astype(BF16)
        rhs = jnp.concatenate([ch["v"] * ch["beta"], ch["kb"] * ch["ecol"]], axis=1)
        sol = rhs + _dot(t_low, rhs.astype(BF16))
        ch["u"] = sol[:, :DN_HEAD_DIM]
        qh = ch["q"]
        ch["wq"] = jnp.concatenate([sol[:, DN_HEAD_DIM:].astype(BF16), (qh * ch["ecol"]).astype(BF16)], axis=0)
        kdt = (ch["k"].T * jnp.exp(ch["glast"] - ch["grow"])).astype(BF16)
        ch["attn_kdt"] = jnp.concatenate([ch["attn"], kdt], axis=0)

    for c in range(n_chunks):
        group = [ch for ch in chains if ch["c"] == c]
        for ch in group:
            ch["proj"] = _dot(ch["wq"], state_ref[ch["idx"]].astype(BF16))
        for ch in group:
            ch["v_new"] = (ch["u"] - ch["proj"][:c_len]).astype(BF16)
        for ch in group:
            idx = ch["idx"]
            upd = _dot(ch["attn_kdt"], ch["v_new"])
            o_ref[ch["b"], ch["rows"], ch["cs"]] = ch["proj"][c_len:] + upd[:c_len]
            state_ref[idx] = state_ref[idx] * jnp.exp(ch["glast"]) + upd[c_len:]


def _mixer_proj_kernel(h_ref, halo_ref, wcat_ref, convw_ref, alog_ref, dtb_ref, poolw_ref, pscale_ref,
                       q_ref, k_ref, v_ref, gb_ref, yp_ref):
    i = pl.program_id(1)
    outs = _project_rows(h_ref[0], halo_ref[0], i == 0, i * h_ref.shape[1],
                         wcat_ref, convw_ref, alog_ref, dtb_ref, poolw_ref, pscale_ref)
    for ref, val in zip((q_ref, k_ref, v_ref, gb_ref, yp_ref), outs):
        ref[0] = val


def _mixer_proj(h, wcat, conv_w, alog_pad, dtb_pad, pool_w, pool_scale):
    bsz, seq, d = h.shape
    tm = min(TOKEN_TILE, seq)
    hb = tm // HALO
    row = lambda w: pl.BlockSpec((1, tm, w), lambda b, i: (b, i, 0))
    out_f = jax.ShapeDtypeStruct((bsz, seq, DN_WIDTH), F32)
    consts = (wcat, conv_w, alog_pad, dtb_pad, pool_w, pool_scale)
    return pl.pallas_call(
        _mixer_proj_kernel,
        grid=(bsz, seq // tm),
        in_specs=[row(d), pl.BlockSpec((1, HALO, d), lambda b, i: (b, jnp.maximum(i * hb - 1, 0), 0))]
                 + [_const_spec(c.shape) for c in consts],
        out_specs=[row(DN_WIDTH), row(DN_WIDTH), row(DN_WIDTH), row(AB_PAD), row(POOL_WIDTH)],
        out_shape=[out_f, out_f, out_f,
                   jax.ShapeDtypeStruct((bsz, seq, AB_PAD), F32),
                   jax.ShapeDtypeStruct((bsz, seq, POOL_WIDTH), BF16)],
        compiler_params=_params("parallel", "parallel"),
        name="mixer_proj",
    )(h, h, *consts)


def _delta_kernel(q_ref, k_ref, v_ref, gb_ref, o_ref, state_ref):
    @pl.when(pl.program_id(0) == 0)
    def _():
        state_ref[...] = jnp.zeros_like(state_ref)

    _delta_stages(q_ref, k_ref, v_ref, gb_ref, o_ref, state_ref)


def _delta_rule(q, k, v, gb):
    bsz, seq, width = q.shape
    rows = min(DELTA_STEP_CHUNKS * DELTA_CHUNK, seq)
    blk = lambda w: pl.BlockSpec((bsz, rows, w), lambda n: (0, n, 0))
    return pl.pallas_call(
        _delta_kernel,
        grid=(seq // rows,),
        in_specs=[blk(width), blk(width), blk(width), blk(AB_PAD)],
        out_specs=blk(width),
        out_shape=jax.ShapeDtypeStruct((bsz, seq, width), F32),
        scratch_shapes=[pltpu.VMEM((bsz * DN_HEADS, DN_HEAD_DIM, DN_HEAD_DIM), F32)],
        compiler_params=_params("arbitrary"),
        name="delta_rule",
    )(q, k, v, gb)


def _mixer_out_kernel(h_ref, o_ref, yp_ref, wz_ref, wgd_ref, wgp_ref, wdn_ref, wpool_ref, wmix_ref,
                      dnw_ref, g_ref, b_ref, out_ref):
    def project(rows):
        hb = h_ref[rows, :].astype(BF16)
        return (_dot(hb, wz_ref[...]), _dot(hb, wgd_ref[...]), _dot(hb, wgp_ref[...]),
                _dot(yp_ref[rows, :], wpool_ref[...]))

    def finish(rows, z, gate_dn, gate_pool, y_pool):
        o = o_ref[rows, :]
        gated = []
        for hh in range(DN_HEADS):
            cs = slice(hh * DN_HEAD_DIM, (hh + 1) * DN_HEAD_DIM)
            oh = o[:, cs]
            oh = oh * lax.rsqrt(jnp.mean(oh * oh, axis=-1, keepdims=True) + RMS_EPS)
            gated.append((oh * dnw_ref[...] * _silu(z[:, cs])).astype(BF16))
        y_dn = _dot(jnp.concatenate(gated, axis=1), wdn_ref[...])
        merged = jax.nn.sigmoid(gate_dn) * y_dn + jax.nn.sigmoid(gate_pool) * y_pool
        mix = _dot(merged.astype(BF16), wmix_ref[...])
        out_ref[rows, :] = _layernorm(ALPHA * h_ref[rows, :] + mix, g_ref[...], b_ref[...])

    groups = _row_groups(h_ref.shape[0])
    ahead = project(groups[0])
    for n, rows in enumerate(groups):
        current = ahead
        if n + 1 < len(groups):
            ahead = project(groups[n + 1])
        finish(rows, *current)


def _mixer_out(h2d, o2d, yp2d, wz, wgd, wgp, wdn, wpool, wmix, dnw, g, b):
    t, d = h2d.shape
    tm = min(TOKEN_TILE, t)
    row = lambda w: pl.BlockSpec((tm, w), lambda i: (i, 0))
    consts = (wz, wgd, wgp, wdn, wpool, wmix, dnw, g, b)
    return pl.pallas_call(
        _mixer_out_kernel,
        grid=(t // tm,),
        in_specs=[row(d), row(DN_WIDTH), row(POOL_WIDTH)] + [_const_spec(c.shape) for c in consts],
        out_specs=row(d),
        out_shape=jax.ShapeDtypeStruct((t, d), F32),
        compiler_params=_params("parallel"),
        name="mixer_out",
    )(h2d, o2d, yp2d, *consts)


def _mem_kv_kernel(m_ref, g_ref, b_ref, wk_ref, wv_ref, k_ref, v_ref):
    m = _layernorm(m_ref[0], g_ref[...], b_ref[...]).astype(BF16)
    k_ref[0] = _dot(m, wk_ref[...]).astype(BF16)
    v_ref[0] = _dot(m, wv_ref[...]).astype(BF16)


def _mem_kv(mem, g, b, wk, wv):
    bsz, n_mem, d = mem.shape
    blk = pl.BlockSpec((1, n_mem, d), lambda i: (i, 0, 0))
    out = jax.ShapeDtypeStruct((bsz, n_mem, d), BF16)
    return pl.pallas_call(
        _mem_kv_kernel,
        grid=(bsz,),
        in_specs=[blk, _const_spec(g.shape), _const_spec(b.shape), _const_spec(wk.shape), _const_spec(wv.shape)],
        out_specs=[blk, blk],
        out_shape=[out, out],
        compiler_params=_params("parallel"),
        name="mem_kv",
    )(mem, g, b, wk, wv)


def _xattn_ln_kernel(x_ref, k_ref, v_ref, wq_ref, wo_ref, g_ref, b_ref, o_ref):
    tm, d = x_ref.shape[1], x_ref.shape[2]
    hd = d // XA_HEADS
    cols = [slice(hh * hd, (hh + 1) * hd) for hh in range(XA_HEADS)]
    groups = _row_groups(tm)
    xs, scores = [], []
    for rows in groups:
        x = x_ref[0, rows, :]
        q = _dot(x.astype(BF16), wq_ref[...])
        xs.append(x)
        scores.append([_dot_nt(q[:, cs].astype(BF16), k_ref[0, :, cs]) * (hd ** -0.5) for cs in cols])
    for rows, x, group_scores in zip(groups, xs, scores):
        heads = []
        for cs, s in zip(cols, group_scores):
            e = jnp.exp(s - jnp.max(s, axis=-1, keepdims=True))
            pr = e * (1.0 / jnp.sum(e, axis=-1, keepdims=True))
            heads.append(_dot(pr.astype(BF16), v_ref[0, :, cs]).astype(BF16))
        xa = _dot(jnp.concatenate(heads, axis=1), wo_ref[...])
        o_ref[0, rows, :] = _layernorm(ALPHA * x + xa, g_ref[...], b_ref[...])


def _xattn_ln(x, k, v, wq, wo, g, b):
    bsz, seq, d = x.shape
    n_mem = k.shape[1]
    tm = min(TOKEN_TILE, seq)
    row = pl.BlockSpec((1, tm, d), lambda bb, i: (bb, i, 0))
    kv = pl.BlockSpec((1, n_mem, d), lambda bb, i: (bb, 0, 0))
    return pl.pallas_call(
        _xattn_ln_kernel,
        grid=(bsz, seq // tm),
        in_specs=[row, kv, kv, _const_spec(wq.shape), _const_spec(wo.shape),
                  _const_spec(g.shape), _const_spec(b.shape)],
        out_specs=row,
        out_shape=jax.ShapeDtypeStruct((bsz, seq, d), F32),
        compiler_params=_params("parallel", "parallel"),
        name="xattn_ln",
    )(x, k, v, wq, wo, g, b)


def _row(vec):
    return vec.reshape(1, -1).astype(F32)


def _pad_lanes(vec):
    return jnp.zeros((1, AB_PAD), F32).at[0, :vec.shape[0]].set(vec.astype(F32))


def kernel(x, mem, ffn1_w_gate, ffn1_w_up, ffn1_w_down, ln1_g, ln1_b, w_in, conv_w, a_log, dt_bias, dn_norm_w, w_dn_branch, pool_w, pool_scale, w_pool_branch, w_mix_out, ln2_g, ln2_b, mem_ln_g, mem_ln_b, xa_wq, xa_wk, xa_wv, xa_wo, ln3_g, ln3_b, ffn2_w_gate, ffn2_w_up, ffn2_w_down, ln4_g, ln4_b):
    bsz, seq, d = x.shape
    t = bsz * seq
    bf = lambda w: w.astype(BF16)

    for l in range(ffn1_w_gate.shape[0]):
        later = [ffn2_w_gate[l], ffn2_w_up[l], ffn2_w_down[l], w_dn_branch[l], w_pool_branch[l], w_mix_out[l],
                 xa_wq[l], xa_wk[l], xa_wv[l], xa_wo[l]]
        x2d, later = _ffn_ln(x.reshape(t, d), bf(ffn1_w_gate[l]), bf(ffn1_w_up[l]), bf(ffn1_w_down[l]),
                             _row(ln1_g[l]), _row(ln1_b[l]), riders=later, w_in_t=jnp.swapaxes(w_in[l], 0, 1))
        f2_gate, f2_up, f2_down, w_dn, w_pool, w_mix, wq, wk, wv, wo, wcat, w_z, w_gdn, w_gpool = later

        q, k, v, gb, ypool = _mixer_proj(x2d.reshape(bsz, seq, d), wcat, conv_w[l].astype(F32),
                                         _pad_lanes(a_log[l]), _pad_lanes(dt_bias[l]),
                                         bf(pool_w[l]), _row(pool_scale[l]))
        o = _delta_rule(q, k, v, gb)
        x2d = _mixer_out(x2d, o.reshape(t, DN_WIDTH), ypool.reshape(t, POOL_WIDTH),
                         w_z, w_gdn, w_gpool, w_dn, w_pool, w_mix,
                         _row(dn_norm_w[l]), _row(ln2_g[l]), _row(ln2_b[l]))

        mk, mv = _mem_kv(mem, _row(mem_ln_g[l]), _row(mem_ln_b[l]), wk, wv)
        x3 = _xattn_ln(x2d.reshape(bsz, seq, d), mk, mv, wq, wo, _row(ln3_g[l]), _row(ln3_b[l]))
        x, _ = _ffn_ln(x3.reshape(t, d), f2_gate, f2_up, f2_down, _row(ln4_g[l]), _row(ln4_b[l]))
        x = x.reshape(bsz, seq, d)
    return x
```

```python
import functools

import jax
import jax.numpy as jnp
from jax import lax
from jax.experimental import pallas as pl
from jax.experimental.pallas import tpu as pltpu

F32 = jnp.float32
BF16 = jnp.bfloat16

DN_HEADS = 4
DN_HEAD_DIM = 128
DN_WIDTH = DN_HEADS * DN_HEAD_DIM
CONV_K = 4
POOL_WINDOWS = (2, 4, 8, 16)
POOL_GROUP_DIM = 128
POOL_WIDTH = len(POOL_WINDOWS) * POOL_GROUP_DIM
XA_HEADS = 4
LN_EPS = 1e-5
RMS_EPS = 1e-6
L2_EPS = 1e-6
DEPTH = 1
ALPHA = (2.0 * DEPTH) ** 0.25

LANES = 128
BF16_ROWS = 16
HALO = 16
DELTA_CHUNK = 128
AB_PAD = LANES
VMEM_LIMIT = 60000 * 1024

TOKEN_TILE = 1024
DELTA_STEP_CHUNKS = 2
ROW_GROUP = 256
MXU_DIM = 256
FF_SLICE = 3 * MXU_DIM


def _dot(a, b):
    return jnp.dot(a, b, preferred_element_type=F32)


def _dot_nt(a, b):
    return lax.dot_general(a, b, (((1,), (1,)), ((), ())), preferred_element_type=F32)


def _layernorm(y, g, b):
    mu = jnp.mean(y, axis=-1, keepdims=True)
    yc = y - mu
    var = jnp.mean(yc * yc, axis=-1, keepdims=True)
    return yc * lax.rsqrt(var + LN_EPS) * g + b


def _silu(x):
    return x * jax.nn.sigmoid(x)


def _row_groups(n_rows):
    size = min(ROW_GROUP, n_rows)
    return [slice(lo, lo + size) for lo in range(0, n_rows, size)]


def _const_spec(shape):
    nd = len(shape)
    return pl.BlockSpec(shape, lambda *_: (0,) * nd, pipeline_mode=pl.Buffered(1))


def _params(*sem):
    return pltpu.CompilerParams(dimension_semantics=sem, vmem_limit_bytes=VMEM_LIMIT)


def _split_w_in_t(blk_t):
    o_qkv = 3 * DN_WIDTH
    o_z = o_qkv + DN_WIDTH
    o_ab = o_z + 2 * DN_HEADS
    o_p = o_ab + POOL_WIDTH
    d = (blk_t.shape[0] - o_p) // 2
    kk = blk_t.shape[1]

    def rows_t(lo, hi):
        return jnp.concatenate([blk_t[r:r + LANES, :].T for r in range(lo, hi, LANES)], axis=1)

    ab = jnp.concatenate([blk_t[o_z:o_ab, :], jnp.zeros((AB_PAD - 2 * DN_HEADS, kk), F32)], axis=0).T
    wcat = jnp.concatenate([rows_t(0, o_qkv), rows_t(o_ab, o_p), ab], axis=1)
    parts = (wcat, rows_t(o_qkv, o_z), rows_t(o_p, o_p + d), rows_t(o_p + d, o_p + 2 * d))
    return [p.astype(BF16) for p in parts]


def _ffn_ln_rows(x, wg_ref, wu_ref, wd_ref, g_ref, b_ref):
    ff = wg_ref.shape[1]
    slices = [slice(lo, min(lo + FF_SLICE, ff)) for lo in range(0, ff, FF_SLICE)]
    xb = x.astype(BF16)
    gate_up = lambda cs: (_dot(xb, wg_ref[:, cs]), _dot(xb, wu_ref[:, cs]))
    y = None
    ahead = gate_up(slices[0])
    for n, cs in enumerate(slices):
        hg, hu = ahead
        if n + 1 < len(slices):
            ahead = gate_up(slices[n + 1])
        part = _dot((_silu(hg) * hu).astype(BF16), wd_ref[cs, :])
        y = part if y is None else y + part
    return _layernorm(ALPHA * x + 0.5 * y, g_ref[...], b_ref[...])


def _ffn_ln_kernel(n_riders, w_in_blocks, x_ref, wg_ref, wu_ref, wd_ref, g_ref, b_ref, *refs):
    n_in = n_riders + int(w_in_blocks > 0)
    rider_in, o_ref, rider_out = refs[:n_in], refs[n_in], refs[n_in + 1:]
    for src, dst in zip(rider_in[:n_riders], rider_out[:n_riders]):
        dst[...] = src[...].astype(BF16)
    if w_in_blocks:
        @pl.when(pl.program_id(0) < w_in_blocks)
        def _():
            for part, dst in zip(_split_w_in_t(rider_in[n_riders][...]), rider_out[n_riders:]):
                dst[...] = part

    for rows in _row_groups(x_ref.shape[0]):
        o_ref[rows, :] = _ffn_ln_rows(x_ref[rows, :], wg_ref, wu_ref, wd_ref, g_ref, b_ref)


def _ffn_ln(x2d, wg, wu, wd, g, b, riders=(), w_in_t=None):
    t, d = x2d.shape
    ff = wg.shape[1]
    tm = min(TOKEN_TILE, t)
    steps = t // tm
    row = pl.BlockSpec((tm, d), lambda i: (i, 0))
    rider_specs = []
    for w in riders:
        rows = w.shape[0] // steps
        assert w.shape[0] == rows * steps and rows % BF16_ROWS == 0, w.shape
        rider_specs.append(pl.BlockSpec((rows, w.shape[1]), lambda i: (i, 0)))
    in_specs, out_specs = list(rider_specs), list(rider_specs)
    out_shape = [jax.ShapeDtypeStruct(w.shape, BF16) for w in riders]
    operands = list(riders)
    w_in_blocks = 0
    if w_in_t is not None:
        w_in_blocks = min(steps, d // LANES)
        kk = d // w_in_blocks
        slab = lambda i: jnp.minimum(i, w_in_blocks - 1)
        in_specs.append(pl.BlockSpec((w_in_t.shape[0], kk), lambda i: (0, slab(i))))
        operands.append(w_in_t)
        for width in (3 * DN_WIDTH + POOL_WIDTH + AB_PAD, DN_WIDTH, d, d):
            out_specs.append(pl.BlockSpec((kk, width), lambda i: (slab(i), 0)))
            out_shape.append(jax.ShapeDtypeStruct((d, width), BF16))
    outs = pl.pallas_call(
        functools.partial(_ffn_ln_kernel, len(riders), w_in_blocks),
        grid=(steps,),
        in_specs=[row, _const_spec((d, ff)), _const_spec((d, ff)), _const_spec((ff, d)),
                  _const_spec((1, d)), _const_spec((1, d))] + in_specs,
        out_specs=[row] + out_specs,
        out_shape=[jax.ShapeDtypeStruct((t, d), F32)] + out_shape,
        compiler_params=_params("parallel"),
        name="ffn_ln",
    )(x2d, wg, wu, wd, g, b, *operands)
    return outs[0], outs[1:]


def _project_rows(h, halo, first, t0, wcat_ref, convw_ref, alog_ref, dtb_ref, poolw_ref, pscale_ref):
    n = h.shape[0]
    qkv_w = 3 * DN_WIDTH
    halo = jnp.where(first, 0.0, halo)
    hcat = jnp.concatenate([halo, h], axis=0).astype(BF16)
    proj = _dot(hcat, wcat_ref[...])

    qkv_pre = proj[:, :qkv_w]
    conv = qkv_pre * convw_ref[CONV_K - 1:CONV_K, :]
    for s in range(1, CONV_K):
        conv = conv + pltpu.roll(qkv_pre, s, 0) * convw_ref[CONV_K - 1 - s:CONV_K - s, :]
    qkv = _silu(conv[HALO:, :])

    parts = []
    for part, scale in ((0, DN_HEAD_DIM ** -0.5), (1, None), (2, None)):
        heads = []
        for hh in range(DN_HEADS):
            lo = part * DN_WIDTH + hh * DN_HEAD_DIM
            seg = qkv[:, lo:lo + DN_HEAD_DIM]
            if part < 2:
                inv = lax.rsqrt(jnp.sum(seg * seg, axis=-1, keepdims=True) + L2_EPS)
                seg = seg * (inv if scale is None else inv * scale)
            heads.append(seg)
        parts.append(jnp.concatenate(heads, axis=1))

    ab = proj[HALO:, qkv_w + POOL_WIDTH:]
    xa = ab + dtb_ref[...]
    softplus = jnp.maximum(xa, 0.0) + jnp.log1p(jnp.exp(-jnp.abs(xa)))
    gval = -jnp.exp(alog_ref[...]) * softplus
    lane = lax.broadcasted_iota(jnp.int32, ab.shape, 1)
    gb = jnp.where(lane < DN_HEADS, gval, jax.nn.sigmoid(ab))

    t = (t0 + lax.broadcasted_iota(jnp.int32, (n, 1), 0) + 1).astype(F32)
    pooled = []
    for gi, win in enumerate(POOL_WINDOWS):
        lo = qkv_w + gi * POOL_GROUP_DIM
        seg = proj[:, lo:lo + POOL_GROUP_DIM]
        wsum = seg
        sh = 1
        while sh < win:
            wsum = wsum + pltpu.roll(wsum, sh, 0)
            sh *= 2
        inv_cnt = 1.0 / jnp.minimum(t, float(win))
        mixed = wsum[HALO:, :] * inv_cnt - seg[HALO:, :]
        y = _dot(mixed.astype(BF16), poolw_ref[gi])
        cs = slice(gi * POOL_GROUP_DIM, (gi + 1) * POOL_GROUP_DIM)
        pooled.append((y * pscale_ref[:, cs]).astype(BF16))
    return parts[0], parts[1], parts[2], gb, jnp.concatenate(pooled, axis=1)


def _delta_stages(q_ref, k_ref, v_ref, gb_ref, o_ref, state_ref):
    c_len = DELTA_CHUNK
    bsz = q_ref.shape[0]
    n_chunks = q_ref.shape[1] // c_len
    ri = lax.broadcasted_iota(jnp.int32, (c_len, c_len), 0)
    ci = lax.broadcasted_iota(jnp.int32, (c_len, c_len), 1)
    incl = ri >= ci
    eye = ri == ci
    level_masks = []
    s = 1
    while s < c_len:
        level_masks.append(((ri // (2 * s)) == (ci // (2 * s))) & ((ri // s) % 2 == 1) & ((ci // s) % 2 == 0))
        s *= 2
    lane = lax.broadcasted_iota(jnp.int32, (c_len, LANES), 1)

    chains = []
    for c in range(n_chunks):
        rows = slice(c * c_len, (c + 1) * c_len)
        for b in range(bsz):
            gbc = gb_ref[b, rows, :]
            g_only = jnp.where(lane < DN_HEADS, gbc, 0.0)
            gc = jnp.dot(incl.astype(F32), g_only, precision=lax.Precision.HIGHEST,
                         preferred_element_type=F32)
            gct = gc.T
            e_gc = jnp.exp(gc)
            for hh in range(DN_HEADS):
                cs = slice(hh * DN_HEAD_DIM, (hh + 1) * DN_HEAD_DIM)
                chains.append(dict(
                    c=c, b=b, idx=b * DN_HEADS + hh, rows=rows, cs=cs,
                    q=q_ref[b, rows, cs], k=k_ref[b, rows, cs], v=v_ref[b, rows, cs],
                    beta=gbc[:, DN_HEADS + hh:DN_HEADS + hh + 1],
                    gcol=gc[:, hh:hh + 1],
                    grow=gct[hh:hh + 1, :],
                    glast=gct[hh:hh + 1, c_len - 1:c_len],
                    ecol=e_gc[:, hh:hh + 1]))

    for ch in chains:
        kh = ch["k"]
        ch["decay"] = jnp.exp(jnp.where(incl, ch["gcol"] - ch["grow"], -jnp.inf))
        ch["kb"] = kh * ch["beta"]
        ch["khb"] = kh.astype(BF16)
    for ch in chains:
        both = _dot_nt(jnp.concatenate([ch["kb"].astype(BF16), ch["q"].astype(BF16)], axis=0), ch["khb"])
        ch["a"] = jnp.where(eye, 0.0, both[:c_len] * ch["decay"])
        ch["attn"] = (both[c_len:] * ch["decay"]).astype(BF16)

    for ch in chains:
        ch["dinv"] = jnp.where(eye, 1.0, 0.0) - jnp.where(level_masks[0], ch["a"], 0.0)
    for m in level_masks[1:]:
        for ch in chains:
            ch["db"] = ch["dinv"].astype(BF16)
            ch["left"] = _dot(ch["db"], jnp.where(m, ch["a"], 0.0).astype(BF16))
        for ch in chains:
            ch["dinv"] = ch["dinv"] - _dot(ch["left"].astype(BF16), ch["db"])

    for ch in chains:
        t_low = jnp.where(eye, 0.0, ch["dinv"]).astype(BF16)
        rhs = jnp.concatenate([ch["v"] * ch["beta"], ch["kb"] * ch["ecol"]], axis=1)
        sol = rhs + _dot(t_low, rhs.astype(BF16))
        ch["u"] = sol[:, :DN_HEAD_DIM]
        qh = ch["q"]
        ch["wq"] = jnp.concatenate([sol[:, DN_HEAD_DIM:].astype(BF16), (qh * ch["ecol"]).astype(BF16)], axis=0)
        kdt = (ch["k"].T * jnp.exp(ch["glast"] - ch["grow"])).astype(BF16)
        ch["attn_kdt"] = jnp.concatenate([ch["attn"], kdt], axis=0)

    for c in range(n_chunks):
        group = [ch for ch in chains if ch["c"] == c]
        for ch in group:
            ch["proj"] = _dot(ch["wq"], state_ref[ch["idx"]].astype(BF16))
        for ch in group:
            ch["v_new"] = (ch["u"] - ch["proj"][:c_len]).astype(BF16)
        for ch in group:
            idx = ch["idx"]
            upd = _dot(ch["attn_kdt"], ch["v_new"])
            o_ref[ch["b"], ch["rows"], ch["cs"]] = ch["proj"][c_len:] + upd[:c_len]
            state_ref[idx] = state_ref[idx] * jnp.exp(ch["glast"]) + upd[c_len:]


def _mixer_proj_kernel(h_ref, halo_ref, wcat_ref, convw_ref, alog_ref, dtb_ref, poolw_ref, pscale_ref,
                       q_ref, k_ref, v_ref, gb_ref, yp_ref):
    i = pl.program_id(1)
    outs = _project_rows(h_ref[0], halo_ref[0], i == 0, i * h_ref.shape[1],
                         wcat_ref, convw_ref, alog_ref, dtb_ref, poolw_ref, pscale_ref)
    for ref, val in zip((q_ref, k_ref, v_ref, gb_ref, yp_ref), outs):
        ref[0] = val


def _mixer_proj(h, wcat, conv_w, alog_pad, dtb_pad, pool_w, pool_scale):
    bsz, seq, d = h.shape
    tm = min(TOKEN_TILE, seq)
    hb = tm // HALO
    row = lambda w: pl.BlockSpec((1, tm, w), lambda b, i: (b, i, 0))
    out_f = jax.ShapeDtypeStruct((bsz, seq, DN_WIDTH), F32)
    consts = (wcat, conv_w, alog_pad, dtb_pad, pool_w, pool_scale)
    return pl.pallas_call(
        _mixer_proj_kernel,
        grid=(bsz, seq // tm),
        in_specs=[row(d), pl.BlockSpec((1, HALO, d), lambda b, i: (b, jnp.maximum(i * hb - 1, 0), 0))]
                 + [_const_spec(c.shape) for c in consts],
        out_specs=[row(DN_WIDTH), row(DN_WIDTH), row(DN_WIDTH), row(AB_PAD), row(POOL_WIDTH)],
        out_shape=[out_f, out_f, out_f,
                   jax.ShapeDtypeStruct((bsz, seq, AB_PAD), F32),
                   jax.ShapeDtypeStruct((bsz, seq, POOL_WIDTH), BF16)],
        compiler_params=_params("parallel", "parallel"),
        name="mixer_proj",
    )(h, h, *consts)


def _delta_kernel(q_ref, k_ref, v_ref, gb_ref, o_ref, state_ref):
    @pl.when(pl.program_id(0) == 0)
    def _():
        state_ref[...] = jnp.zeros_like(state_ref)

    _delta_stages(q_ref, k_ref, v_ref, gb_ref, o_ref, state_ref)


def _delta_rule(q, k, v, gb):
    bsz, seq, width = q.shape
    rows = min(DELTA_STEP_CHUNKS * DELTA_CHUNK, seq)
    blk = lambda w: pl.BlockSpec((bsz, rows, w), lambda n: (0, n, 0))
    return pl.pallas_call(
        _delta_kernel,
        grid=(seq // rows,),
        in_specs=[blk(width), blk(width), blk(width), blk(AB_PAD)],
        out_specs=blk(width),
        out_shape=jax.ShapeDtypeStruct((bsz, seq, width), F32),
        scratch_shapes=[pltpu.VMEM((bsz * DN_HEADS, DN_HEAD_DIM, DN_HEAD_DIM), F32)],
        compiler_params=_params("arbitrary"),
        name="delta_rule",
    )(q, k, v, gb)


def _mixer_out_kernel(h_ref, o_ref, yp_ref, wz_ref, wgd_ref, wgp_ref, wdn_ref, wpool_ref, wmix_ref,
                      dnw_ref, g_ref, b_ref, out_ref):
    def project(rows):
        hb = h_ref[rows, :].astype(BF16)
        return (_dot(hb, wz_ref[...]), _dot(hb, wgd_ref[...]), _dot(hb, wgp_ref[...]),
                _dot(yp_ref[rows, :], wpool_ref[...]))

    def finish(rows, z, gate_dn, gate_pool, y_pool):
        o = o_ref[rows, :]
        gated = []
        for hh in range(DN_HEADS):
            cs = slice(hh * DN_HEAD_DIM, (hh + 1) * DN_HEAD_DIM)
            oh = o[:, cs]
            oh = oh * lax.rsqrt(jnp.mean(oh * oh, axis=-1, keepdims=True) + RMS_EPS)
            gated.append((oh * dnw_ref[...] * _silu(z[:, cs])).astype(BF16))
        y_dn = _dot(jnp.concatenate(gated, axis=1), wdn_ref[...])
        merged = jax.nn.sigmoid(gate_dn) * y_dn + jax.nn.sigmoid(gate_pool) * y_pool
        mix = _dot(merged.astype(BF16), wmix_ref[...])
        out_ref[rows, :] = _layernorm(ALPHA * h_ref[rows, :] + mix, g_ref[...], b_ref[...])

    groups = _row_groups(h_ref.shape[0])
    ahead = project(groups[0])
    for n, rows in enumerate(groups):
        current = ahead
        if n + 1 < len(groups):
            ahead = project(groups[n + 1])
        finish(rows, *current)


def _mixer_out(h2d, o2d, yp2d, wz, wgd, wgp, wdn, wpool, wmix, dnw, g, b):
    t, d = h2d.shape
    tm = min(TOKEN_TILE, t)
    row = lambda w: pl.BlockSpec((tm, w), lambda i: (i, 0))
    consts = (wz, wgd, wgp, wdn, wpool, wmix, dnw, g, b)
    return pl.pallas_call(
        _mixer_out_kernel,
        grid=(t // tm,),
        in_specs=[row(d), row(DN_WIDTH), row(POOL_WIDTH)] + [_const_spec(c.shape) for c in consts],
        out_specs=row(d),
        out_shape=jax.ShapeDtypeStruct((t, d), F32),
        compiler_params=_params("parallel"),
        name="mixer_out",
    )(h2d, o2d, yp2d, *consts)


def _mem_kv_kernel(m_ref, g_ref, b_ref, wk_ref, wv_ref, k_ref, v_ref):
    m = _layernorm(m_ref[0], g_ref[...], b_ref[...]).astype(BF16)
    k_ref[0] = _dot(m, wk_ref[...]).astype(BF16)
    v_ref[0] = _dot(m, wv_ref[...]).astype(BF16)


def _mem_kv(mem, g, b, wk, wv):
    bsz, n_mem, d = mem.shape
    blk = pl.BlockSpec((1, n_mem, d), lambda i: (i, 0, 0))
    out = jax.ShapeDtypeStruct((bsz, n_mem, d), BF16)
    return pl.pallas_call(
        _mem_kv_kernel,
        grid=(bsz,),
        in_specs=[blk, _const_spec(g.shape), _const_spec(b.shape), _const_spec(wk.shape), _const_spec(wv.shape)],
        out_specs=[blk, blk],
        out_shape=[out, out],
        compiler_params=_params("parallel"),
        name="mem_kv",
    )(mem, g, b, wk, wv)


def _xattn_ffn_kernel(x_ref, k_ref, v_ref, wq_ref, wo_ref, g3_ref, b3_ref,
                      wg_ref, wu_ref, wd_ref, g4_ref, b4_ref, o_ref):
    tm, d = x_ref.shape[1], x_ref.shape[2]
    hd = d // XA_HEADS
    cols = [slice(hh * hd, (hh + 1) * hd) for hh in range(XA_HEADS)]
    groups = _row_groups(tm)

    def attend(x, group_scores):
        heads = []
        for cs, s in zip(cols, group_scores):
            e = jnp.exp(s - jnp.max(s, axis=-1, keepdims=True))
            pr = e * (1.0 / jnp.sum(e, axis=-1, keepdims=True))
            heads.append(_dot(pr.astype(BF16), v_ref[0, :, cs]).astype(BF16))
        xa = _dot(jnp.concatenate(heads, axis=1), wo_ref[...])
        return _layernorm(ALPHA * x + xa, g3_ref[...], b3_ref[...])

    xs, scores = [], []
    for rows in groups:
        x = x_ref[0, rows, :]
        q = _dot(x.astype(BF16), wq_ref[...])
        xs.append(x)
        scores.append([_dot_nt(q[:, cs].astype(BF16), k_ref[0, :, cs]) * (hd ** -0.5) for cs in cols])
    ahead = attend(xs[0], scores[0])
    for n, rows in enumerate(groups):
        x3 = ahead
        if n + 1 < len(groups):
            ahead = attend(xs[n + 1], scores[n + 1])
        o_ref[0, rows, :] = _ffn_ln_rows(x3, wg_ref, wu_ref, wd_ref, g4_ref, b4_ref)


def _xattn_ffn(x, k, v, wq, wo, g3, b3, wg, wu, wd, g4, b4):
    bsz, seq, d = x.shape
    n_mem = k.shape[1]
    tm = min(TOKEN_TILE, seq)
    row = pl.BlockSpec((1, tm, d), lambda bb, i: (bb, i, 0))
    kv = pl.BlockSpec((1, n_mem, d), lambda bb, i: (bb, 0, 0))
    consts = (wq, wo, g3, b3, wg, wu, wd, g4, b4)
    return pl.pallas_call(
        _xattn_ffn_kernel,
        grid=(bsz, seq // tm),
        in_specs=[row, kv, kv] + [_const_spec(c.shape) for c in consts],
        out_specs=row,
        out_shape=jax.ShapeDtypeStruct((bsz, seq, d), F32),
        compiler_params=_params("parallel", "parallel"),
        name="xattn_ffn",
    )(x, k, v, *consts)


def _row(vec):
    return vec.reshape(1, -1).astype(F32)


def _pad_lanes(vec):
    return jnp.zeros((1, AB_PAD), F32).at[0, :vec.shape[0]].set(vec.astype(F32))


def kernel(x, mem, ffn1_w_gate, ffn1_w_up, ffn1_w_down, ln1_g, ln1_b, w_in, conv_w, a_log, dt_bias, dn_norm_w, w_dn_branch, pool_w, pool_scale, w_pool_branch, w_mix_out, ln2_g, ln2_b, mem_ln_g, mem_ln_b, xa_wq, xa_wk, xa_wv, xa_wo, ln3_g, ln3_b, ffn2_w_gate, ffn2_w_up, ffn2_w_down, ln4_g, ln4_b):
    bsz, seq, d = x.shape
    t = bsz * seq
    bf = lambda w: w.astype(BF16)

    for l in range(ffn1_w_gate.shape[0]):
        later = [ffn2_w_gate[l], ffn2_w_up[l], ffn2_w_down[l], w_dn_branch[l], w_pool_branch[l], w_mix_out[l],
                 xa_wq[l], xa_wk[l], xa_wv[l], xa_wo[l]]
        x2d, later = _ffn_ln(x.reshape(t, d), bf(ffn1_w_gate[l]), bf(ffn1_w_up[l]), bf(ffn1_w_down[l]),
                             _row(ln1_g[l]), _row(ln1_b[l]), riders=later, w_in_t=jnp.swapaxes(w_in[l], 0, 1))
        f2_gate, f2_up, f2_down, w_dn, w_pool, w_mix, wq, wk, wv, wo, wcat, w_z, w_gdn, w_gpool = later

        q, k, v, gb, ypool = _mixer_proj(x2d.reshape(bsz, seq, d), wcat, conv_w[l].astype(F32),
                                         _pad_lanes(a_log[l]), _pad_lanes(dt_bias[l]),
                                         bf(pool_w[l]), _row(pool_scale[l]))
        o = _delta_rule(q, k, v, gb)
        x2d = _mixer_out(x2d, o.reshape(t, DN_WIDTH), ypool.reshape(t, POOL_WIDTH),
                         w_z, w_gdn, w_gpool, w_dn, w_pool, w_mix,
                         _row(dn_norm_w[l]), _row(ln2_g[l]), _row(ln2_b[l]))

        mk, mv = _mem_kv(mem, _row(mem_ln_g[l]), _row(mem_ln_b[l]), wk, wv)
        x = _xattn_ffn(x2d.reshape(bsz, seq, d), mk, mv, wq, wo, _row(ln3_g[l]), _row(ln3_b[l]),
                       f2_gate, f2_up, f2_down, _row(ln4_g[l]), _row(ln4_b[l]))
    return x
```

```python
import functools

import jax
import jax.numpy as jnp
from jax import lax
from jax.experimental import pallas as pl
from jax.experimental.pallas import tpu as pltpu

F32 = jnp.float32
BF16 = jnp.bfloat16

DN_HEADS = 4
DN_HEAD_DIM = 128
DN_WIDTH = DN_HEADS * DN_HEAD_DIM
CONV_K = 4
POOL_WINDOWS = (2, 4, 8, 16)
POOL_GROUP_DIM = 128
POOL_WIDTH = len(POOL_WINDOWS) * POOL_GROUP_DIM
XA_HEADS = 4
LN_EPS = 1e-5
RMS_EPS = 1e-6
L2_EPS = 1e-6
DEPTH = 1
ALPHA = (2.0 * DEPTH) ** 0.25

LANES = 128
BF16_ROWS = 16
HALO = 16
DELTA_CHUNK = 128
AB_PAD = LANES
VMEM_LIMIT = 60000 * 1024

TOKEN_TILE = 1024
DELTA_STEP_CHUNKS = 2
ROW_GROUP = 256
MXU_DIM = 256
FF_SLICE = 3 * MXU_DIM


def _dot(a, b):
    return jnp.dot(a, b, preferred_element_type=F32)


def _dot_nt(a, b):
    return lax.dot_general(a, b, (((1,), (1,)), ((), ())), preferred_element_type=F32)


def _layernorm(y, g, b):
    mu = jnp.mean(y, axis=-1, keepdims=True)
    yc = y - mu
    var = jnp.mean(yc * yc, axis=-1, keepdims=True)
    return yc * lax.rsqrt(var + LN_EPS) * g + b


def _silu(x):
    return x * jax.nn.sigmoid(x)


def _row_groups(n_rows):
    size = min(ROW_GROUP, n_rows)
    return [slice(lo, lo + size) for lo in range(0, n_rows, size)]


def _const_spec(shape):
    nd = len(shape)
    return pl.BlockSpec(shape, lambda *_: (0,) * nd, pipeline_mode=pl.Buffered(1))


def _params(*sem):
    return pltpu.CompilerParams(dimension_semantics=sem, vmem_limit_bytes=VMEM_LIMIT)


def _split_w_in_t(blk_t):
    o_qkv = 3 * DN_WIDTH
    o_z = o_qkv + DN_WIDTH
    o_ab = o_z + 2 * DN_HEADS
    o_p = o_ab + POOL_WIDTH
    d = (blk_t.shape[0] - o_p) // 2
    kk = blk_t.shape[1]

    def rows_t(lo, hi):
        return jnp.concatenate([blk_t[r:r + LANES, :].T for r in range(lo, hi, LANES)], axis=1)

    ab = jnp.concatenate([blk_t[o_z:o_ab, :], jnp.zeros((AB_PAD - 2 * DN_HEADS, kk), F32)], axis=0).T
    wcat = jnp.concatenate([rows_t(0, o_qkv), rows_t(o_ab, o_p), ab], axis=1)
    parts = (wcat, rows_t(o_qkv, o_z), rows_t(o_p, o_p + d), rows_t(o_p + d, o_p + 2 * d))
    return [p.astype(BF16) for p in parts]


def _ffn_ln_rows(x, wg_ref, wu_ref, wd_ref, g_ref, b_ref):
    ff = wg_ref.shape[1]
    slices = [slice(lo, min(lo + FF_SLICE, ff)) for lo in range(0, ff, FF_SLICE)]
    xb = x.astype(BF16)
    gate_up = lambda cs: (_dot(xb, wg_ref[:, cs]), _dot(xb, wu_ref[:, cs]))
    y = None
    ahead = gate_up(slices[0])
    for n, cs in enumerate(slices):
        hg, hu = ahead
        if n + 1 < len(slices):
            ahead = gate_up(slices[n + 1])
        part = _dot((_silu(hg) * hu).astype(BF16), wd_ref[cs, :])
        y = part if y is None else y + part
    return _layernorm(ALPHA * x + 0.5 * y, g_ref[...], b_ref[...])


def _ffn_ln_kernel(n_riders, w_in_blocks, x_ref, wg_ref, wu_ref, wd_ref, g_ref, b_ref, *refs):
    n_in = n_riders + int(w_in_blocks > 0)
    rider_in, o_ref, rider_out = refs[:n_in], refs[n_in], refs[n_in + 1:]
    for src, dst in zip(rider_in[:n_riders], rider_out[:n_riders]):
        dst[...] = src[...].astype(BF16)
    if w_in_blocks:
        @pl.when(pl.program_id(0) < w_in_blocks)
        def _():
            for part, dst in zip(_split_w_in_t(rider_in[n_riders][...]), rider_out[n_riders:]):
                dst[...] = part

    for rows in _row_groups(x_ref.shape[0]):
        o_ref[rows, :] = _ffn_ln_rows(x_ref[rows, :], wg_ref, wu_ref, wd_ref, g_ref, b_ref)


def _ffn_ln(x2d, wg, wu, wd, g, b, riders=(), w_in_t=None):
    t, d = x2d.shape
    ff = wg.shape[1]
    tm = min(TOKEN_TILE, t)
    steps = t // tm
    row = pl.BlockSpec((tm, d), lambda i: (i, 0))
    rider_specs = []
    for w in riders:
        rows = w.shape[0] // steps
        assert w.shape[0] == rows * steps and rows % BF16_ROWS == 0, w.shape
        rider_specs.append(pl.BlockSpec((rows, w.shape[1]), lambda i: (i, 0)))
    in_specs, out_specs = list(rider_specs), list(rider_specs)
    out_shape = [jax.ShapeDtypeStruct(w.shape, BF16) for w in riders]
    operands = list(riders)
    w_in_blocks = 0
    if w_in_t is not None:
        w_in_blocks = min(steps, d // LANES)
        kk = d // w_in_blocks
        slab = lambda i: jnp.minimum(i, w_in_blocks - 1)
        in_specs.append(pl.BlockSpec((w_in_t.shape[0], kk), lambda i: (0, slab(i))))
        operands.append(w_in_t)
        for width in (3 * DN_WIDTH + POOL_WIDTH + AB_PAD, DN_WIDTH, d, d):
            out_specs.append(pl.BlockSpec((kk, width), lambda i: (slab(i), 0)))
            out_shape.append(jax.ShapeDtypeStruct((d, width), BF16))
    outs = pl.pallas_call(
        functools.partial(_ffn_ln_kernel, len(riders), w_in_blocks),
        grid=(steps,),
        in_specs=[row, _const_spec((d, ff)), _const_spec((d, ff)), _const_spec((ff, d)),
                  _const_spec((1, d)), _const_spec((1, d))] + in_specs,
        out_specs=[row] + out_specs,
        out_shape=[jax.ShapeDtypeStruct((t, d), F32)] + out_shape,
        compiler_params=_params("parallel"),
        name="ffn_ln",
    )(x2d, wg, wu, wd, g, b, *operands)
    return outs[0], outs[1:]


def _project_rows(h, halo, first, t0, wcat_ref, convw_ref, alog_ref, dtb_ref, poolw_ref, pscale_ref):
    n = h.shape[0]
    qkv_w = 3 * DN_WIDTH
    halo = jnp.where(first, 0.0, halo)
    hcat = jnp.concatenate([halo, h], axis=0).astype(BF16)
    proj = _dot(hcat, wcat_ref[...])

    qkv_pre = proj[:, :qkv_w]
    conv = qkv_pre * convw_ref[CONV_K - 1:CONV_K, :]
    for s in range(1, CONV_K):
        conv = conv + pltpu.roll(qkv_pre, s, 0) * convw_ref[CONV_K - 1 - s:CONV_K - s, :]
    qkv = _silu(conv[HALO:, :])

    parts = []
    for part, scale in ((0, DN_HEAD_DIM ** -0.5), (1, None), (2, None)):
        heads = []
        for hh in range(DN_HEADS):
            lo = part * DN_WIDTH + hh * DN_HEAD_DIM
            seg = qkv[:, lo:lo + DN_HEAD_DIM]
            if part < 2:
                inv = lax.rsqrt(jnp.sum(seg * seg, axis=-1, keepdims=True) + L2_EPS)
                seg = seg * (inv if scale is None else inv * scale)
            heads.append(seg)
        parts.append(jnp.concatenate(heads, axis=1))

    ab = proj[HALO:, qkv_w + POOL_WIDTH:]
    xa = ab + dtb_ref[...]
    softplus = jnp.maximum(xa, 0.0) + jnp.log1p(jnp.exp(-jnp.abs(xa)))
    gval = -jnp.exp(alog_ref[...]) * softplus
    lane = lax.broadcasted_iota(jnp.int32, ab.shape, 1)
    gb = jnp.where(lane < DN_HEADS, gval, jax.nn.sigmoid(ab))

    t = (t0 + lax.broadcasted_iota(jnp.int32, (n, 1), 0) + 1).astype(F32)
    pooled = []
    for gi, win in enumerate(POOL_WINDOWS):
        lo = qkv_w + gi * POOL_GROUP_DIM
        seg = proj[:, lo:lo + POOL_GROUP_DIM]
        wsum = seg
        sh = 1
        while sh < win:
            wsum = wsum + pltpu.roll(wsum, sh, 0)
            sh *= 2
        inv_cnt = 1.0 / jnp.minimum(t, float(win))
        mixed = wsum[HALO:, :] * inv_cnt - seg[HALO:, :]
        y = _dot(mixed.astype(BF16), poolw_ref[gi])
        cs = slice(gi * POOL_GROUP_DIM, (gi + 1) * POOL_GROUP_DIM)
        pooled.append((y * pscale_ref[:, cs]).astype(BF16))
    return parts[0], parts[1], parts[2], gb, jnp.concatenate(pooled, axis=1)


def _delta_stages(q_ref, k_ref, v_ref, gb_ref, o_ref, state_ref):
    c_len = DELTA_CHUNK
    bsz = q_ref.shape[0]
    n_chunks = q_ref.shape[1] // c_len
    ri = lax.broadcasted_iota(jnp.int32, (c_len, c_len), 0)
    ci = lax.broadcasted_iota(jnp.int32, (c_len, c_len), 1)
    incl = ri >= ci
    eye = ri == ci
    level_masks = []
    s = 1
    while s < c_len:
        level_masks.append(((ri // (2 * s)) == (ci // (2 * s))) & ((ri // s) % 2 == 1) & ((ci // s) % 2 == 0))
        s *= 2
    lane = lax.broadcasted_iota(jnp.int32, (c_len, LANES), 1)

    chains = []
    for c in range(n_chunks):
        rows = slice(c * c_len, (c + 1) * c_len)
        for b in range(bsz):
            gbc = gb_ref[b, rows, :]
            g_only = jnp.where(lane < DN_HEADS, gbc, 0.0)
            gc = jnp.dot(incl.astype(F32), g_only, precision=lax.Precision.HIGHEST,
                         preferred_element_type=F32)
            gct = gc.T
            e_gc = jnp.exp(gc)
            for hh in range(DN_HEADS):
                cs = slice(hh * DN_HEAD_DIM, (hh + 1) * DN_HEAD_DIM)
                chains.append(dict(
                    c=c, b=b, idx=b * DN_HEADS + hh, rows=rows, cs=cs,
                    q=q_ref[b, rows, cs], k=k_ref[b, rows, cs], v=v_ref[b, rows, cs],
                    beta=gbc[:, DN_HEADS + hh:DN_HEADS + hh + 1],
                    gcol=gc[:, hh:hh + 1],
                    grow=gct[hh:hh + 1, :],
                    glast=gct[hh:hh + 1, c_len - 1:c_len],
                    ecol=e_gc[:, hh:hh + 1]))

    for ch in chains:
        kh = ch["k"]
        ch["decay"] = jnp.exp(jnp.where(incl, ch["gcol"] - ch["grow"], -jnp.inf))
        ch["kb"] = kh * ch["beta"]
        ch["khb"] = kh.astype(BF16)
    for ch in chains:
        both = _dot_nt(jnp.concatenate([ch["kb"].astype(BF16), ch["q"].astype(BF16)], axis=0), ch["khb"])
        ch["a"] = jnp.where(eye, 0.0, both[:c_len] * ch["decay"])
        ch["attn"] = (both[c_len:] * ch["decay"]).astype(BF16)

    for ch in chains:
        ch["dinv"] = jnp.where(eye, 1.0, 0.0) - jnp.where(level_masks[0], ch["a"], 0.0)
    for m in level_masks[1:]:
        for ch in chains:
            ch["db"] = ch["dinv"].astype(BF16)
            ch["left"] = _dot(ch["db"], jnp.where(m, ch["a"], 0.0).astype(BF16))
        for ch in chains:
            ch["dinv"] = ch["dinv"] - _dot(ch["left"].astype(BF16), ch["db"])

    for ch in chains:
        t_low = jnp.where(eye, 0.0, ch["dinv"]).astype(BF16)
        rhs = jnp.concatenate([ch["v"] * ch["beta"], ch["kb"] * ch["ecol"]], axis=1)
        sol = rhs + _dot(t_low, rhs.astype(BF16))
        ch["u"] = sol[:, :DN_HEAD_DIM]
        qh = ch["q"]
        ch["wq"] = jnp.concatenate([sol[:, DN_HEAD_DIM:].astype(BF16), (qh * ch["ecol"]).astype(BF16)], axis=0)
        kdt = (ch["k"].T * jnp.exp(ch["glast"] - ch["grow"])).astype(BF16)
        ch["attn_kdt"] = jnp.concatenate([ch["attn"], kdt], axis=0)

    for c in range(n_chunks):
        group = [ch for ch in chains if ch["c"] == c]
        for ch in group:
            ch["proj"] = _dot(ch["wq"], state_ref[ch["idx"]].astype(BF16))
        for ch in group:
            ch["v_new"] = (ch["u"] - ch["proj"][:c_len]).astype(BF16)
        for ch in group:
            idx = ch["idx"]
            upd = _dot(ch["attn_kdt"], ch["v_new"])
            o_ref[ch["b"], ch["rows"], ch["cs"]] = ch["proj"][c_len:] + upd[:c_len]
            state_ref[idx] = state_ref[idx] * jnp.exp(ch["glast"]) + upd[c_len:]


def _mixer_proj_kernel(h_ref, halo_ref, wcat_ref, convw_ref, alog_ref, dtb_ref, poolw_ref, pscale_ref,
                       q_ref, k_ref, v_ref, gb_ref, yp_ref):
    i = pl.program_id(1)
    outs = _project_rows(h_ref[0], halo_ref[0], i == 0, i * h_ref.shape[1],
                         wcat_ref, convw_ref, alog_ref, dtb_ref, poolw_ref, pscale_ref)
    for ref, val in zip((q_ref, k_ref, v_ref, gb_ref, yp_ref), outs):
        ref[0] = val


def _mixer_proj(h, wcat, conv_w, alog_pad, dtb_pad, pool_w, pool_scale):
    bsz, seq, d = h.shape
    tm = min(TOKEN_TILE, seq)
    hb = tm // HALO
    row = lambda w: pl.BlockSpec((1, tm, w), lambda b, i: (b, i, 0))
    out_f = jax.ShapeDtypeStruct((bsz, seq, DN_WIDTH), F32)
    consts = (wcat, conv_w, alog_pad, dtb_pad, pool_w, pool_scale)
    return pl.pallas_call(
        _mixer_proj_kernel,
        grid=(bsz, seq // tm),
        in_specs=[row(d), pl.BlockSpec((1, HALO, d), lambda b, i: (b, jnp.maximum(i * hb - 1, 0), 0))]
                 + [_const_spec(c.shape) for c in consts],
        out_specs=[row(DN_WIDTH), row(DN_WIDTH), row(DN_WIDTH), row(AB_PAD), row(POOL_WIDTH)],
        out_shape=[out_f, out_f, out_f,
                   jax.ShapeDtypeStruct((bsz, seq, AB_PAD), F32),
                   jax.ShapeDtypeStruct((bsz, seq, POOL_WIDTH), BF16)],
        compiler_params=_params("parallel", "parallel"),
        name="mixer_proj",
    )(h, h, *consts)


def _delta_kernel(q_ref, k_ref, v_ref, gb_ref, o_ref, state_ref):
    @pl.when(pl.program_id(0) == 0)
    def _():
        state_ref[...] = jnp.zeros_like(state_ref)

    _delta_stages(q_ref, k_ref, v_ref, gb_ref, o_ref, state_ref)


def _delta_rule(q, k, v, gb):
    bsz, seq, width = q.shape
    rows = min(DELTA_STEP_CHUNKS * DELTA_CHUNK, seq)
    blk = lambda w: pl.BlockSpec((bsz, rows, w), lambda n: (0, n, 0))
    return pl.pallas_call(
        _delta_kernel,
        grid=(seq // rows,),
        in_specs=[blk(width), blk(width), blk(width), blk(AB_PAD)],
        out_specs=blk(width),
        out_shape=jax.ShapeDtypeStruct((bsz, seq, width), F32),
        scratch_shapes=[pltpu.VMEM((bsz * DN_HEADS, DN_HEAD_DIM, DN_HEAD_DIM), F32)],
        compiler_params=_params("arbitrary"),
        name="delta_rule",
    )(q, k, v, gb)


def _mixer_out_kernel(h_ref, o_ref, yp_ref, wz_ref, wgd_ref, wgp_ref, wdn_ref, wpool_ref, wmix_ref,
                      dnw_ref, g_ref, b_ref, out_ref):
    def project(rows):
        hb = h_ref[rows, :].astype(BF16)
        return (_dot(hb, wz_ref[...]), _dot(hb, wgd_ref[...]), _dot(hb, wgp_ref[...]),
                _dot(yp_ref[rows, :], wpool_ref[...]))

    def finish(rows, z, gate_dn, gate_pool, y_pool):
        o = o_ref[rows, :]
        gated = []
        for hh in range(DN_HEADS):
            cs = slice(hh * DN_HEAD_DIM, (hh + 1) * DN_HEAD_DIM)
            oh = o[:, cs]
            oh = oh * lax.rsqrt(jnp.mean(oh * oh, axis=-1, keepdims=True) + RMS_EPS)
            gated.append((oh * dnw_ref[...] * _silu(z[:, cs])).astype(BF16))
        y_dn = _dot(jnp.concatenate(gated, axis=1), wdn_ref[...])
        merged = jax.nn.sigmoid(gate_dn) * y_dn + jax.nn.sigmoid(gate_pool) * y_pool
        mix = _dot(merged.astype(BF16), wmix_ref[...])
        out_ref[rows, :] = _layernorm(ALPHA * h_ref[rows, :] + mix, g_ref[...], b_ref[...])

    groups = _row_groups(h_ref.shape[0])
    ahead = project(groups[0])
    for n, rows in enumerate(groups):
        current = ahead
        if n + 1 < len(groups):
            ahead = project(groups[n + 1])
        finish(rows, *current)


def _mixer_out(h2d, o2d, yp2d, wz, wgd, wgp, wdn, wpool, wmix, dnw, g, b):
    t, d = h2d.shape
    tm = min(TOKEN_TILE, t)
    row = lambda w: pl.BlockSpec((tm, w), lambda i: (i, 0))
    consts = (wz, wgd, wgp, wdn, wpool, wmix, dnw, g, b)
    return pl.pallas_call(
        _mixer_out_kernel,
        grid=(t // tm,),
        in_specs=[row(d), row(DN_WIDTH), row(POOL_WIDTH)] + [_const_spec(c.shape) for c in consts],
        out_specs=row(d),
        out_shape=jax.ShapeDtypeStruct((t, d), F32),
        compiler_params=_params("parallel"),
        name="mixer_out",
    )(h2d, o2d, yp2d, *consts)


def _xattn_ffn_kernel(x_ref, mem_ref, gm_ref, bm_ref, wk_ref, wv_ref, wq_ref, wo_ref, g3_ref, b3_ref,
                      wg_ref, wu_ref, wd_ref, g4_ref, b4_ref, o_ref, k_ref, v_ref):
    @pl.when(pl.program_id(1) == 0)
    def _():
        m = _layernorm(mem_ref[0], gm_ref[...], bm_ref[...]).astype(BF16)
        k_ref[0] = _dot(m, wk_ref[...]).astype(BF16)
        v_ref[0] = _dot(m, wv_ref[...]).astype(BF16)

    tm, d = x_ref.shape[1], x_ref.shape[2]
    hd = d // XA_HEADS
    cols = [slice(hh * hd, (hh + 1) * hd) for hh in range(XA_HEADS)]
    groups = _row_groups(tm)

    def attend(x, group_scores):
        heads = []
        for cs, s in zip(cols, group_scores):
            e = jnp.exp(s - jnp.max(s, axis=-1, keepdims=True))
            pr = e * (1.0 / jnp.sum(e, axis=-1, keepdims=True))
            heads.append(_dot(pr.astype(BF16), v_ref[0, :, cs]).astype(BF16))
        xa = _dot(jnp.concatenate(heads, axis=1), wo_ref[...])
        return _layernorm(ALPHA * x + xa, g3_ref[...], b3_ref[...])

    xs, scores = [], []
    for rows in groups:
        x = x_ref[0, rows, :]
        q = _dot(x.astype(BF16), wq_ref[...])
        xs.append(x)
        scores.append([_dot_nt(q[:, cs].astype(BF16), k_ref[0, :, cs]) * (hd ** -0.5) for cs in cols])
    ahead = attend(xs[0], scores[0])
    for n, rows in enumerate(groups):
        x3 = ahead
        if n + 1 < len(groups):
            ahead = attend(xs[n + 1], scores[n + 1])
        o_ref[0, rows, :] = _ffn_ln_rows(x3, wg_ref, wu_ref, wd_ref, g4_ref, b4_ref)


def _xattn_ffn(x, mem, gm, bm, wk, wv, wq, wo, g3, b3, wg, wu, wd, g4, b4):
    bsz, seq, d = x.shape
    n_mem = mem.shape[1]
    tm = min(TOKEN_TILE, seq)
    row = pl.BlockSpec((1, tm, d), lambda bb, i: (bb, i, 0))
    consts = (gm, bm, wk, wv, wq, wo, g3, b3, wg, wu, wd, g4, b4)
    return pl.pallas_call(
        _xattn_ffn_kernel,
        grid=(bsz, seq // tm),
        in_specs=[row, pl.BlockSpec((1, n_mem, d), lambda bb, i: (bb, 0, 0))]
                 + [_const_spec(c.shape) for c in consts],
        out_specs=row,
        out_shape=jax.ShapeDtypeStruct((bsz, seq, d), F32),
        scratch_shapes=[pltpu.VMEM((1, n_mem, d), BF16), pltpu.VMEM((1, n_mem, d), BF16)],
        compiler_params=_params("arbitrary", "arbitrary"),
        name="xattn_ffn",
    )(x, mem, *consts)


def _row(vec):
    return vec.reshape(1, -1).astype(F32)


def _pad_lanes(vec):
    return jnp.zeros((1, AB_PAD), F32).at[0, :vec.shape[0]].set(vec.astype(F32))


def kernel(x, mem, ffn1_w_gate, ffn1_w_up, ffn1_w_down, ln1_g, ln1_b, w_in, conv_w, a_log, dt_bias, dn_norm_w, w_dn_branch, pool_w, pool_scale, w_pool_branch, w_mix_out, ln2_g, ln2_b, mem_ln_g, mem_ln_b, xa_wq, xa_wk, xa_wv, xa_wo, ln3_g, ln3_b, ffn2_w_gate, ffn2_w_up, ffn2_w_down, ln4_g, ln4_b):
    bsz, seq, d = x.shape
    t = bsz * seq
    bf = lambda w: w.astype(BF16)

    for l in range(ffn1_w_gate.shape[0]):
        later = [ffn2_w_gate[l], ffn2_w_up[l], ffn2_w_down[l], w_dn_branch[l], w_pool_branch[l], w_mix_out[l],
                 xa_wq[l], xa_wk[l], xa_wv[l], xa_wo[l]]
        x2d, later = _ffn_ln(x.reshape(t, d), bf(ffn1_w_gate[l]), bf(ffn1_w_up[l]), bf(ffn1_w_down[l]),
                             _row(ln1_g[l]), _row(ln1_b[l]), riders=later, w_in_t=jnp.swapaxes(w_in[l], 0, 1))
        f2_gate, f2_up, f2_down, w_dn, w_pool, w_mix, wq, wk, wv, wo, wcat, w_z, w_gdn, w_gpool = later

        q, k, v, gb, ypool = _mixer_proj(x2d.reshape(bsz, seq, d), wcat, conv_w[l].astype(F32),
                                         _pad_lanes(a_log[l]), _pad_lanes(dt_bias[l]),
                                         bf(pool_w[l]), _row(pool_scale[l]))
        o = _delta_rule(q, k, v, gb)
        x2d = _mixer_out(x2d, o.reshape(t, DN_WIDTH), ypool.reshape(t, POOL_WIDTH),
                         w_z, w_gdn, w_gpool, w_dn, w_pool, w_mix,
                         _row(dn_norm_w[l]), _row(ln2_g[l]), _row(ln2_b[l]))

        x = _xattn_ffn(x2d.reshape(bsz, seq, d), mem, _row(mem_ln_g[l]), _row(mem_ln_b[l]), wk, wv,
                       wq, wo, _row(ln3_g[l]), _row(ln3_b[l]),
                       f2_gate, f2_up, f2_down, _row(ln4_g[l]), _row(ln4_b[l]))
    return x
```

```python
import functools

import jax
import jax.numpy as jnp
from jax import lax
from jax.experimental import pallas as pl
from jax.experimental.pallas import tpu as pltpu

F32 = jnp.float32
BF16 = jnp.bfloat16

DN_HEADS = 4
DN_HEAD_DIM = 128
DN_WIDTH = DN_HEADS * DN_HEAD_DIM
CONV_K = 4
POOL_WINDOWS = (2, 4, 8, 16)
POOL_GROUP_DIM = 128
POOL_WIDTH = len(POOL_WINDOWS) * POOL_GROUP_DIM
XA_HEADS = 4
LN_EPS = 1e-5
RMS_EPS = 1e-6
L2_EPS = 1e-6
DEPTH = 1
ALPHA = (2.0 * DEPTH) ** 0.25

LANES = 128
BF16_ROWS = 16
HALO = 16
DELTA_CHUNK = 128
AB_PAD = LANES
VMEM_LIMIT = 60000 * 1024

TOKEN_TILE = 1024
DELTA_STEP_CHUNKS = 2
ROW_GROUP = 256
MXU_DIM = 256
FF_SLICE = 3 * MXU_DIM


def _dot(a, b):
    return jnp.dot(a, b, preferred_element_type=F32)


def _dot_nt(a, b):
    return lax.dot_general(a, b, (((1,), (1,)), ((), ())), preferred_element_type=F32)


def _layernorm(y, g, b):
    mu = jnp.mean(y, axis=-1, keepdims=True)
    yc = y - mu
    var = jnp.mean(yc * yc, axis=-1, keepdims=True)
    return yc * lax.rsqrt(var + LN_EPS) * g + b


def _silu(x):
    return x * jax.nn.sigmoid(x)


def _row_groups(n_rows):
    size = min(ROW_GROUP, n_rows)
    return [slice(lo, lo + size) for lo in range(0, n_rows, size)]


def _const_spec(shape):
    nd = len(shape)
    return pl.BlockSpec(shape, lambda *_: (0,) * nd, pipeline_mode=pl.Buffered(1))


def _rider_specs(riders, steps):
    specs = []
    for w in riders:
        rows = w.shape[0] // steps
        assert w.shape[0] == rows * steps and rows % BF16_ROWS == 0, w.shape
        specs.append(pl.BlockSpec((rows, w.shape[1]), lambda i: (i, 0)))
    return specs


def _params(*sem):
    return pltpu.CompilerParams(dimension_semantics=sem, vmem_limit_bytes=VMEM_LIMIT)


def _split_w_in_t(blk_t):
    o_qkv = 3 * DN_WIDTH
    o_z = o_qkv + DN_WIDTH
    o_ab = o_z + 2 * DN_HEADS
    o_p = o_ab + POOL_WIDTH
    d = (blk_t.shape[0] - o_p) // 2
    kk = blk_t.shape[1]

    def rows_t(lo, hi):
        return jnp.concatenate([blk_t[r:r + LANES, :].T for r in range(lo, hi, LANES)], axis=1)

    ab = jnp.concatenate([blk_t[o_z:o_ab, :], jnp.zeros((AB_PAD - 2 * DN_HEADS, kk), F32)], axis=0).T
    wcat = jnp.concatenate([rows_t(0, o_qkv), rows_t(o_ab, o_p), ab], axis=1)
    parts = (wcat, rows_t(o_qkv, o_z), rows_t(o_p, o_p + d), rows_t(o_p + d, o_p + 2 * d))
    return [p.astype(BF16) for p in parts]


def _ffn_ln_rows(x, wg_ref, wu_ref, wd_ref, g_ref, b_ref):
    ff = wg_ref.shape[1]
    slices = [slice(lo, min(lo + FF_SLICE, ff)) for lo in range(0, ff, FF_SLICE)]
    xb = x.astype(BF16)
    gate_up = lambda cs: (_dot(xb, wg_ref[:, cs]), _dot(xb, wu_ref[:, cs]))
    y = None
    ahead = gate_up(slices[0])
    for n, cs in enumerate(slices):
        hg, hu = ahead
        if n + 1 < len(slices):
            ahead = gate_up(slices[n + 1])
        part = _dot((_silu(hg) * hu).astype(BF16), wd_ref[cs, :])
        y = part if y is None else y + part
    return _layernorm(ALPHA * x + 0.5 * y, g_ref[...], b_ref[...])


def _ffn_ln_kernel(n_riders, w_in_blocks, x_ref, wg_ref, wu_ref, wd_ref, g_ref, b_ref, *refs):
    n_in = n_riders + int(w_in_blocks > 0)
    rider_in, o_ref, rider_out = refs[:n_in], refs[n_in], refs[n_in + 1:]
    for src, dst in zip(rider_in[:n_riders], rider_out[:n_riders]):
        dst[...] = src[...].astype(BF16)
    if w_in_blocks:
        @pl.when(pl.program_id(0) < w_in_blocks)
        def _():
            for part, dst in zip(_split_w_in_t(rider_in[n_riders][...]), rider_out[n_riders:]):
                dst[...] = part

    for rows in _row_groups(x_ref.shape[0]):
        o_ref[rows, :] = _ffn_ln_rows(x_ref[rows, :], wg_ref, wu_ref, wd_ref, g_ref, b_ref)


def _ffn_ln(x2d, wg, wu, wd, g, b, riders=(), w_in_t=None):
    t, d = x2d.shape
    ff = wg.shape[1]
    tm = min(TOKEN_TILE, t)
    steps = t // tm
    row = pl.BlockSpec((tm, d), lambda i: (i, 0))
    rider_specs = _rider_specs(riders, steps)
    in_specs, out_specs = list(rider_specs), list(rider_specs)
    out_shape = [jax.ShapeDtypeStruct(w.shape, BF16) for w in riders]
    operands = list(riders)
    w_in_blocks = 0
    if w_in_t is not None:
        w_in_blocks = min(steps, d // LANES)
        kk = d // w_in_blocks
        slab = lambda i: jnp.minimum(i, w_in_blocks - 1)
        in_specs.append(pl.BlockSpec((w_in_t.shape[0], kk), lambda i: (0, slab(i))))
        operands.append(w_in_t)
        for width in (3 * DN_WIDTH + POOL_WIDTH + AB_PAD, DN_WIDTH, d, d):
            out_specs.append(pl.BlockSpec((kk, width), lambda i: (slab(i), 0)))
            out_shape.append(jax.ShapeDtypeStruct((d, width), BF16))
    outs = pl.pallas_call(
        functools.partial(_ffn_ln_kernel, len(riders), w_in_blocks),
        grid=(steps,),
        in_specs=[row, _const_spec((d, ff)), _const_spec((d, ff)), _const_spec((ff, d)),
                  _const_spec((1, d)), _const_spec((1, d))] + in_specs,
        out_specs=[row] + out_specs,
        out_shape=[jax.ShapeDtypeStruct((t, d), F32)] + out_shape,
        compiler_params=_params("parallel"),
        name="ffn_ln",
    )(x2d, wg, wu, wd, g, b, *operands)
    return outs[0], outs[1:]


def _project_rows(h, halo, first, t0, wcat_ref, convw_ref, alog_ref, dtb_ref, poolw_ref, pscale_ref):
    n = h.shape[0]
    qkv_w = 3 * DN_WIDTH
    halo = jnp.where(first, 0.0, halo)
    hcat = jnp.concatenate([halo, h], axis=0).astype(BF16)
    proj = _dot(hcat, wcat_ref[...])

    qkv_pre = proj[:, :qkv_w]
    conv = qkv_pre * convw_ref[CONV_K - 1:CONV_K, :]
    for s in range(1, CONV_K):
        conv = conv + pltpu.roll(qkv_pre, s, 0) * convw_ref[CONV_K - 1 - s:CONV_K - s, :]
    qkv = _silu(conv[HALO:, :])

    parts = []
    for part, scale in ((0, DN_HEAD_DIM ** -0.5), (1, None), (2, None)):
        heads = []
        for hh in range(DN_HEADS):
            lo = part * DN_WIDTH + hh * DN_HEAD_DIM
            seg = qkv[:, lo:lo + DN_HEAD_DIM]
            if part < 2:
                inv = lax.rsqrt(jnp.sum(seg * seg, axis=-1, keepdims=True) + L2_EPS)
                seg = seg * (inv if scale is None else inv * scale)
            heads.append(seg)
        parts.append(jnp.concatenate(heads, axis=1))

    ab = proj[HALO:, qkv_w + POOL_WIDTH:]
    xa = ab + dtb_ref[...]
    softplus = jnp.maximum(xa, 0.0) + jnp.log1p(jnp.exp(-jnp.abs(xa)))
    gval = -jnp.exp(alog_ref[...]) * softplus
    lane = lax.broadcasted_iota(jnp.int32, ab.shape, 1)
    gb = jnp.where(lane < DN_HEADS, gval, jax.nn.sigmoid(ab))

    t = (t0 + lax.broadcasted_iota(jnp.int32, (n, 1), 0) + 1).astype(F32)
    pooled = []
    for gi, win in enumerate(POOL_WINDOWS):
        lo = qkv_w + gi * POOL_GROUP_DIM
        seg = proj[:, lo:lo + POOL_GROUP_DIM]
        wsum = seg
        sh = 1
        while sh < win:
            wsum = wsum + pltpu.roll(wsum, sh, 0)
            sh *= 2
        inv_cnt = 1.0 / jnp.minimum(t, float(win))
        mixed = wsum[HALO:, :] * inv_cnt - seg[HALO:, :]
        y = _dot(mixed.astype(BF16), poolw_ref[gi])
        cs = slice(gi * POOL_GROUP_DIM, (gi + 1) * POOL_GROUP_DIM)
        pooled.append((y * pscale_ref[:, cs]).astype(BF16))
    return parts[0], parts[1], parts[2], gb, jnp.concatenate(pooled, axis=1)


def _delta_stages(q_ref, k_ref, v_ref, gb_ref, o_ref, state_ref):
    c_len = DELTA_CHUNK
    bsz = q_ref.shape[0]
    n_chunks = q_ref.shape[1] // c_len
    ri = lax.broadcasted_iota(jnp.int32, (c_len, c_len), 0)
    ci = lax.broadcasted_iota(jnp.int32, (c_len, c_len), 1)
    incl = ri >= ci
    eye = ri == ci
    level_masks = []
    s = 1
    while s < c_len:
        level_masks.append(((ri // (2 * s)) == (ci // (2 * s))) & ((ri // s) % 2 == 1) & ((ci // s) % 2 == 0))
        s *= 2
    lane = lax.broadcasted_iota(jnp.int32, (c_len, LANES), 1)

    chains = []
    for c in range(n_chunks):
        rows = slice(c * c_len, (c + 1) * c_len)
        for b in range(bsz):
            gbc = gb_ref[b, rows, :]
            g_only = jnp.where(lane < DN_HEADS, gbc, 0.0)
            gc = jnp.dot(incl.astype(F32), g_only, precision=lax.Precision.HIGHEST,
                         preferred_element_type=F32)
            gct = gc.T
            e_gc = jnp.exp(gc)
            for hh in range(DN_HEADS):
                cs = slice(hh * DN_HEAD_DIM, (hh + 1) * DN_HEAD_DIM)
                chains.append(dict(
                    c=c, b=b, idx=b * DN_HEADS + hh, rows=rows, cs=cs,
                    q=q_ref[b, rows, cs], k=k_ref[b, rows, cs], v=v_ref[b, rows, cs],
                    beta=gbc[:, DN_HEADS + hh:DN_HEADS + hh + 1],
                    gcol=gc[:, hh:hh + 1],
                    grow=gct[hh:hh + 1, :],
                    glast=gct[hh:hh + 1, c_len - 1:c_len],
                    ecol=e_gc[:, hh:hh + 1]))

    for ch in chains:
        kh = ch["k"]
        ch["decay"] = jnp.exp(jnp.where(incl, ch["gcol"] - ch["grow"], -jnp.inf))
        ch["kb"] = kh * ch["beta"]
        ch["khb"] = kh.astype(BF16)
    for ch in chains:
        both = _dot_nt(jnp.concatenate([ch["kb"].astype(BF16), ch["q"].astype(BF16)], axis=0), ch["khb"])
        ch["a"] = jnp.where(eye, 0.0, both[:c_len] * ch["decay"])
        ch["attn"] = (both[c_len:] * ch["decay"]).astype(BF16)

    for ch in chains:
        ch["dinv"] = jnp.where(eye, 1.0, 0.0) - jnp.where(level_masks[0], ch["a"], 0.0)
    for m in level_masks[1:]:
        for ch in chains:
            ch["db"] = ch["dinv"].astype(BF16)
            ch["left"] = _dot(ch["db"], jnp.where(m, ch["a"], 0.0).astype(BF16))
        for ch in chains:
            ch["dinv"] = ch["dinv"] - _dot(ch["left"].astype(BF16), ch["db"])

    for ch in chains:
        t_low = jnp.where(eye, 0.0, ch["dinv"]).astype(BF16)
        rhs = jnp.concatenate([ch["v"] * ch["beta"], ch["kb"] * ch["ecol"]], axis=1)
        sol = rhs + _dot(t_low, rhs.astype(BF16))
        ch["u"] = sol[:, :DN_HEAD_DIM]
        qh = ch["q"]
        ch["wq"] = jnp.concatenate([sol[:, DN_HEAD_DIM:].astype(BF16), (qh * ch["ecol"]).astype(BF16)], axis=0)
        kdt = (ch["k"].T * jnp.exp(ch["glast"] - ch["grow"])).astype(BF16)
        ch["attn_kdt"] = jnp.concatenate([ch["attn"], kdt], axis=0)

    for c in range(n_chunks):
        group = [ch for ch in chains if ch["c"] == c]
        for ch in group:
            ch["proj"] = _dot(ch["wq"], state_ref[ch["idx"]].astype(BF16))
        for ch in group:
            ch["v_new"] = (ch["u"] - ch["proj"][:c_len]).astype(BF16)
        for ch in group:
            idx = ch["idx"]
            upd = _dot(ch["attn_kdt"], ch["v_new"])
            o_ref[ch["b"], ch["rows"], ch["cs"]] = ch["proj"][c_len:] + upd[:c_len]
            state_ref[idx] = state_ref[idx] * jnp.exp(ch["glast"]) + upd[c_len:]


def _mixer_proj_kernel(h_ref, halo_ref, wcat_ref, convw_ref, alog_ref, dtb_ref, poolw_ref, pscale_ref,
                       q_ref, k_ref, v_ref, gb_ref, yp_ref):
    i = pl.program_id(1)
    outs = _project_rows(h_ref[0], halo_ref[0], i == 0, i * h_ref.shape[1],
                         wcat_ref, convw_ref, alog_ref, dtb_ref, poolw_ref, pscale_ref)
    for ref, val in zip((q_ref, k_ref, v_ref, gb_ref, yp_ref), outs):
        ref[0] = val


def _mixer_proj(h, wcat, conv_w, alog_pad, dtb_pad, pool_w, pool_scale):
    bsz, seq, d = h.shape
    tm = min(TOKEN_TILE, seq)
    hb = tm // HALO
    row = lambda w: pl.BlockSpec((1, tm, w), lambda b, i: (b, i, 0))
    out_f = jax.ShapeDtypeStruct((bsz, seq, DN_WIDTH), F32)
    consts = (wcat, conv_w, alog_pad, dtb_pad, pool_w, pool_scale)
    return pl.pallas_call(
        _mixer_proj_kernel,
        grid=(bsz, seq // tm),
        in_specs=[row(d), pl.BlockSpec((1, HALO, d), lambda b, i: (b, jnp.maximum(i * hb - 1, 0), 0))]
                 + [_const_spec(c.shape) for c in consts],
        out_specs=[row(DN_WIDTH), row(DN_WIDTH), row(DN_WIDTH), row(AB_PAD), row(POOL_WIDTH)],
        out_shape=[out_f, out_f, out_f,
                   jax.ShapeDtypeStruct((bsz, seq, AB_PAD), F32),
                   jax.ShapeDtypeStruct((bsz, seq, POOL_WIDTH), BF16)],
        compiler_params=_params("parallel", "parallel"),
        name="mixer_proj",
    )(h, h, *consts)


def _delta_kernel(n_riders, q_ref, k_ref, v_ref, gb_ref, *refs):
    rider_in, o_ref, rider_out, state_ref = refs[:n_riders], refs[n_riders], refs[n_riders + 1:-1], refs[-1]

    @pl.when(pl.program_id(0) == 0)
    def _():
        state_ref[...] = jnp.zeros_like(state_ref)

    for src, dst in zip(rider_in, rider_out):
        dst[...] = src[...].astype(BF16)
    _delta_stages(q_ref, k_ref, v_ref, gb_ref, o_ref, state_ref)


def _delta_rule(q, k, v, gb, riders=()):
    bsz, seq, width = q.shape
    rows = min(DELTA_STEP_CHUNKS * DELTA_CHUNK, seq)
    blk = lambda w: pl.BlockSpec((bsz, rows, w), lambda n: (0, n, 0))
    rider_specs = _rider_specs(riders, seq // rows)
    outs = pl.pallas_call(
        functools.partial(_delta_kernel, len(riders)),
        grid=(seq // rows,),
        in_specs=[blk(width), blk(width), blk(width), blk(AB_PAD)] + rider_specs,
        out_specs=[blk(width)] + rider_specs,
        out_shape=[jax.ShapeDtypeStruct((bsz, seq, width), F32)]
                  + [jax.ShapeDtypeStruct(w.shape, BF16) for w in riders],
        scratch_shapes=[pltpu.VMEM((bsz * DN_HEADS, DN_HEAD_DIM, DN_HEAD_DIM), F32)],
        compiler_params=_params("arbitrary"),
        name="delta_rule",
    )(q, k, v, gb, *riders)
    return outs[0], outs[1:]


def _mixer_out_kernel(h_ref, o_ref, yp_ref, wz_ref, wgd_ref, wgp_ref, wdn_ref, wpool_ref, wmix_ref,
                      dnw_ref, g_ref, b_ref, out_ref):
    def project(rows):
        hb = h_ref[rows, :].astype(BF16)
        return (_dot(hb, wz_ref[...]), _dot(hb, wgd_ref[...]), _dot(hb, wgp_ref[...]),
                _dot(yp_ref[rows, :], wpool_ref[...]))

    def finish(rows, z, gate_dn, gate_pool, y_pool):
        o = o_ref[rows, :]
        gated = []
        for hh in range(DN_HEADS):
            cs = slice(hh * DN_HEAD_DIM, (hh + 1) * DN_HEAD_DIM)
            oh = o[:, cs]
            oh = oh * lax.rsqrt(jnp.mean(oh * oh, axis=-1, keepdims=True) + RMS_EPS)
            gated.append((oh * dnw_ref[...] * _silu(z[:, cs])).astype(BF16))
        y_dn = _dot(jnp.concatenate(gated, axis=1), wdn_ref[...])
        merged = jax.nn.sigmoid(gate_dn) * y_dn + jax.nn.sigmoid(gate_pool) * y_pool
        mix = _dot(merged.astype(BF16), wmix_ref[...])
        out_ref[rows, :] = _layernorm(ALPHA * h_ref[rows, :] + mix, g_ref[...], b_ref[...])

    groups = _row_groups(h_ref.shape[0])
    ahead = project(groups[0])
    for n, rows in enumerate(groups):
        current = ahead
        if n + 1 < len(groups):
            ahead = project(groups[n + 1])
        finish(rows, *current)


def _mixer_out(h2d, o2d, yp2d, wz, wgd, wgp, wdn, wpool, wmix, dnw, g, b):
    t, d = h2d.shape
    tm = min(TOKEN_TILE, t)
    row = lambda w: pl.BlockSpec((tm, w), lambda i: (i, 0))
    consts = (wz, wgd, wgp, wdn, wpool, wmix, dnw, g, b)
    return pl.pallas_call(
        _mixer_out_kernel,
        grid=(t // tm,),
        in_specs=[row(d), row(DN_WIDTH), row(POOL_WIDTH)] + [_const_spec(c.shape) for c in consts],
        out_specs=row(d),
        out_shape=jax.ShapeDtypeStruct((t, d), F32),
        compiler_params=_params("parallel"),
        name="mixer_out",
    )(h2d, o2d, yp2d, *consts)


def _xattn_ffn_kernel(x_ref, mem_ref, gm_ref, bm_ref, wk_ref, wv_ref, wq_ref, wo_ref, g3_ref, b3_ref,
                      wg_ref, wu_ref, wd_ref, g4_ref, b4_ref, o_ref, k_ref, v_ref):
    @pl.when(pl.program_id(1) == 0)
    def _():
        m = _layernorm(mem_ref[0], gm_ref[...], bm_ref[...]).astype(BF16)
        k_ref[0] = _dot(m, wk_ref[...]).astype(BF16)
        v_ref[0] = _dot(m, wv_ref[...]).astype(BF16)

    tm, d = x_ref.shape[1], x_ref.shape[2]
    hd = d // XA_HEADS
    cols = [slice(hh * hd, (hh + 1) * hd) for hh in range(XA_HEADS)]
    groups = _row_groups(tm)

    def attend(x, group_scores):
        heads = []
        for cs, s in zip(cols, group_scores):
            e = jnp.exp(s - jnp.max(s, axis=-1, keepdims=True))
            pr = e * (1.0 / jnp.sum(e, axis=-1, keepdims=True))
            heads.append(_dot(pr.astype(BF16), v_ref[0, :, cs]).astype(BF16))
        xa = _dot(jnp.concatenate(heads, axis=1), wo_ref[...])
        return _layernorm(ALPHA * x + xa, g3_ref[...], b3_ref[...])

    xs, scores = [], []
    for rows in groups:
        x = x_ref[0, rows, :]
        q = _dot(x.astype(BF16), wq_ref[...])
        xs.append(x)
        scores.append([_dot_nt(q[:, cs].astype(BF16), k_ref[0, :, cs]) * (hd ** -0.5) for cs in cols])
    ahead = attend(xs[0], scores[0])
    for n, rows in enumerate(groups):
        x3 = ahead
        if n + 1 < len(groups):
            ahead = attend(xs[n + 1], scores[n + 1])
        o_ref[0, rows, :] = _ffn_ln_rows(x3, wg_ref, wu_ref, wd_ref, g4_ref, b4_ref)


def _xattn_ffn(x, mem, gm, bm, wk, wv, wq, wo, g3, b3, wg, wu, wd, g4, b4):
    bsz, seq, d = x.shape
    n_mem = mem.shape[1]
    tm = min(TOKEN_TILE, seq)
    row = pl.BlockSpec((1, tm, d), lambda bb, i: (bb, i, 0))
    consts = (gm, bm, wk, wv, wq, wo, g3, b3, wg, wu, wd, g4, b4)
    return pl.pallas_call(
        _xattn_ffn_kernel,
        grid=(bsz, seq // tm),
        in_specs=[row, pl.BlockSpec((1, n_mem, d), lambda bb, i: (bb, 0, 0))]
                 + [_const_spec(c.shape) for c in consts],
        out_specs=row,
        out_shape=jax.ShapeDtypeStruct((bsz, seq, d), F32),
        scratch_shapes=[pltpu.VMEM((1, n_mem, d), BF16), pltpu.VMEM((1, n_mem, d), BF16)],
        compiler_params=_params("arbitrary", "arbitrary"),
        name="xattn_ffn",
    )(x, mem, *consts)


def _row(vec):
    return vec.reshape(1, -1).astype(F32)


def _pad_lanes(vec):
    return jnp.zeros((1, AB_PAD), F32).at[0, :vec.shape[0]].set(vec.astype(F32))


def kernel(x, mem, ffn1_w_gate, ffn1_w_up, ffn1_w_down, ln1_g, ln1_b, w_in, conv_w, a_log, dt_bias, dn_norm_w, w_dn_branch, pool_w, pool_scale, w_pool_branch, w_mix_out, ln2_g, ln2_b, mem_ln_g, mem_ln_b, xa_wq, xa_wk, xa_wv, xa_wo, ln3_g, ln3_b, ffn2_w_gate, ffn2_w_up, ffn2_w_down, ln4_g, ln4_b):
    bsz, seq, d = x.shape
    t = bsz * seq
    bf = lambda w: w.astype(BF16)

    for l in range(ffn1_w_gate.shape[0]):
        x2d, (wcat, w_z, w_gdn, w_gpool) = _ffn_ln(
            x.reshape(t, d), bf(ffn1_w_gate[l]), bf(ffn1_w_up[l]), bf(ffn1_w_down[l]),
            _row(ln1_g[l]), _row(ln1_b[l]), w_in_t=jnp.swapaxes(w_in[l], 0, 1))

        q, k, v, gb, ypool = _mixer_proj(x2d.reshape(bsz, seq, d), wcat, conv_w[l].astype(F32),
                                         _pad_lanes(a_log[l]), _pad_lanes(dt_bias[l]),
                                         bf(pool_w[l]), _row(pool_scale[l]))
        later = [ffn2_w_gate[l], ffn2_w_up[l], ffn2_w_down[l], w_dn_branch[l], w_pool_branch[l], w_mix_out[l],
                 xa_wq[l], xa_wk[l], xa_wv[l], xa_wo[l]]
        o, (f2_gate, f2_up, f2_down, w_dn, w_pool, w_mix, wq, wk, wv, wo) = _delta_rule(q, k, v, gb, riders=later)
        x2d = _mixer_out(x2d, o.reshape(t, DN_WIDTH), ypool.reshape(t, POOL_WIDTH),
                         w_z, w_gdn, w_gpool, w_dn, w_pool, w_mix,
                         _row(dn_norm_w[l]), _row(ln2_g[l]), _row(ln2_b[l]))

        x = _xattn_ffn(x2d.reshape(bsz, seq, d), mem, _row(mem_ln_g[l]), _row(mem_ln_b[l]), wk, wv,
                       wq, wo, _row(ln3_g[l]), _row(ln3_b[l]),
                       f2_gate, f2_up, f2_down, _row(ln4_g[l]), _row(ln4_b[l]))
    return x
```

```python
import functools

import jax
import jax.numpy as jnp
from jax import lax
from jax.experimental import pallas as pl
from jax.experimental.pallas import tpu as pltpu

F32 = jnp.float32
BF16 = jnp.bfloat16

DN_HEADS = 4
DN_HEAD_DIM = 128
DN_WIDTH = DN_HEADS * DN_HEAD_DIM
CONV_K = 4
POOL_WINDOWS = (2, 4, 8, 16)
POOL_GROUP_DIM = 128
POOL_WIDTH = len(POOL_WINDOWS) * POOL_GROUP_DIM
XA_HEADS = 4
LN_EPS = 1e-5
RMS_EPS = 1e-6
L2_EPS = 1e-6
DEPTH = 1
ALPHA = (2.0 * DEPTH) ** 0.25

LANES = 128
BF16_ROWS = 16
HALO = 16
DELTA_CHUNK = 128
AB_PAD = LANES
VMEM_LIMIT = 60000 * 1024

TOKEN_TILE = 1024
DELTA_STEP_CHUNKS = 2
ROW_GROUP = 256
MXU_DIM = 256
FF_SLICE = 3 * MXU_DIM
WEIGHT_CHUNKS = 16
WEIGHT_STAGE_SLOTS = 8


def _dot(a, b):
    return jnp.dot(a, b, preferred_element_type=F32)


def _dot_nt(a, b):
    return lax.dot_general(a, b, (((1,), (1,)), ((), ())), preferred_element_type=F32)


def _layernorm(y, g, b):
    mu = jnp.mean(y, axis=-1, keepdims=True)
    yc = y - mu
    var = jnp.mean(yc * yc, axis=-1, keepdims=True)
    return yc * lax.rsqrt(var + LN_EPS) * g + b


def _silu(x):
    return x * jax.nn.sigmoid(x)


def _row_groups(n_rows):
    size = min(ROW_GROUP, n_rows)
    return [slice(lo, lo + size) for lo in range(0, n_rows, size)]


def _const_spec(shape):
    nd = len(shape)
    return pl.BlockSpec(shape, lambda *_: (0,) * nd, pipeline_mode=pl.Buffered(1))


def _rider_specs(riders, steps):
    specs = []
    for w in riders:
        rows = w.shape[0] // steps
        assert w.shape[0] == rows * steps and rows % BF16_ROWS == 0, w.shape
        specs.append(pl.BlockSpec((rows, w.shape[1]), lambda i: (i, 0)))
    return specs


def _params(*sem):
    return pltpu.CompilerParams(dimension_semantics=sem, vmem_limit_bytes=VMEM_LIMIT)


def _split_w_in_t(blk_t):
    o_qkv = 3 * DN_WIDTH
    o_z = o_qkv + DN_WIDTH
    o_ab = o_z + 2 * DN_HEADS
    o_p = o_ab + POOL_WIDTH
    d = (blk_t.shape[0] - o_p) // 2
    kk = blk_t.shape[1]

    def rows_t(lo, hi):
        return jnp.concatenate([blk_t[r:r + LANES, :].T for r in range(lo, hi, LANES)], axis=1)

    ab = jnp.concatenate([blk_t[o_z:o_ab, :], jnp.zeros((AB_PAD - 2 * DN_HEADS, kk), F32)], axis=0).T
    wcat = jnp.concatenate([rows_t(0, o_qkv), rows_t(o_ab, o_p), ab], axis=1)
    parts = (wcat, rows_t(o_qkv, o_z), rows_t(o_p, o_p + d), rows_t(o_p + d, o_p + 2 * d))
    return [p.astype(BF16) for p in parts]


def _ffn_ln_rows(x, wg_ref, wu_ref, wd_ref, g_ref, b_ref):
    ff = wg_ref.shape[1]
    slices = [slice(lo, min(lo + FF_SLICE, ff)) for lo in range(0, ff, FF_SLICE)]
    xb = x.astype(BF16)
    gate_up = lambda cs: (_dot(xb, wg_ref[:, cs]), _dot(xb, wu_ref[:, cs]))
    y = None
    ahead = gate_up(slices[0])
    for n, cs in enumerate(slices):
        hg, hu = ahead
        if n + 1 < len(slices):
            ahead = gate_up(slices[n + 1])
        part = _dot((_silu(hg) * hu).astype(BF16), wd_ref[cs, :])
        y = part if y is None else y + part
    return _layernorm(ALPHA * x + 0.5 * y, g_ref[...], b_ref[...])


def _fetch_cast(jobs):
    chunks = []
    for src, dst, stage, sem in jobs:
        slots, rows = stage.shape[0], stage.shape[1]
        assert (src.shape[0] // rows) % slots == 0, (src.shape, stage.shape)
        for c in range(src.shape[0] // rows):
            window = pl.ds(c * rows, rows)
            copy = pltpu.make_async_copy(src.at[window, :], stage.at[c % slots], sem.at[c % slots])
            chunks.append((copy, dst, window, stage, c % slots))
    ahead = min(job[2].shape[0] for job in jobs) - 1
    for copy, *_ in chunks[:ahead]:
        copy.start()
    for n, (copy, dst, window, stage, slot) in enumerate(chunks):
        if n + ahead < len(chunks):
            chunks[n + ahead][0].start()
        copy.wait()
        dst[window, :] = stage[slot].astype(BF16)


def _ffn_ln_kernel(w_in_blocks, x_ref, wg_hbm, wu_hbm, wd_hbm, g_ref, b_ref, w_in_ref, o_ref,
                   wcat_ref, wz_ref, wgd_ref, wgp_ref, wg_ref, wu_ref, wd_ref, stage_a, stage_b, sem_a, sem_b):
    @pl.when(pl.program_id(0) == 0)
    def _():
        _fetch_cast([(wg_hbm, wg_ref, stage_a, sem_a), (wu_hbm, wu_ref, stage_a, sem_a),
                     (wd_hbm, wd_ref, stage_b, sem_b)])

    @pl.when(pl.program_id(0) < w_in_blocks)
    def _():
        for part, dst in zip(_split_w_in_t(w_in_ref[...]), (wcat_ref, wz_ref, wgd_ref, wgp_ref)):
            dst[...] = part

    for rows in _row_groups(x_ref.shape[0]):
        o_ref[rows, :] = _ffn_ln_rows(x_ref[rows, :], wg_ref, wu_ref, wd_ref, g_ref, b_ref)


def _ffn_ln(x2d, wg, wu, wd, g, b, w_in_t):
    t, d = x2d.shape
    ff = wg.shape[1]
    tm = min(TOKEN_TILE, t)
    steps = t // tm
    row = pl.BlockSpec((tm, d), lambda i: (i, 0))
    w_in_blocks = min(steps, d // LANES)
    kk = d // w_in_blocks
    slab = lambda i: jnp.minimum(i, w_in_blocks - 1)
    widths = (3 * DN_WIDTH + POOL_WIDTH + AB_PAD, DN_WIDTH, d, d)
    assert wg.dtype == wu.dtype == wd.dtype == F32
    chunk = lambda n_rows: n_rows // WEIGHT_CHUNKS
    assert d % WEIGHT_CHUNKS == 0 and ff % WEIGHT_CHUNKS == 0 and chunk(d) % BF16_ROWS == 0 \
        and chunk(ff) % BF16_ROWS == 0, (d, ff)
    scratch = [pltpu.VMEM((d, ff), BF16), pltpu.VMEM((d, ff), BF16), pltpu.VMEM((ff, d), BF16),
               pltpu.VMEM((WEIGHT_STAGE_SLOTS, chunk(d), ff), F32),
               pltpu.VMEM((WEIGHT_STAGE_SLOTS, chunk(ff), d), F32),
               pltpu.SemaphoreType.DMA((WEIGHT_STAGE_SLOTS,)), pltpu.SemaphoreType.DMA((WEIGHT_STAGE_SLOTS,))]
    outs = pl.pallas_call(
        functools.partial(_ffn_ln_kernel, w_in_blocks),
        grid=(steps,),
        in_specs=[row] + [pl.BlockSpec(memory_space=pl.ANY)] * 3 + [_const_spec((1, d)), _const_spec((1, d))]
                 + [pl.BlockSpec((w_in_t.shape[0], kk), lambda i: (0, slab(i)))],
        out_specs=[row] + [pl.BlockSpec((kk, w), lambda i: (slab(i), 0)) for w in widths],
        out_shape=[jax.ShapeDtypeStruct((t, d), F32)] + [jax.ShapeDtypeStruct((d, w), BF16) for w in widths],
        scratch_shapes=scratch,
        compiler_params=_params("arbitrary"),
        name="ffn_ln",
    )(x2d, wg, wu, wd, g, b, w_in_t)
    return outs[0], outs[1:]


def _project_rows(h, halo, first, t0, wcat_ref, convw_ref, alog_ref, dtb_ref, poolw_ref, pscale_ref):
    n = h.shape[0]
    qkv_w = 3 * DN_WIDTH
    halo = jnp.where(first, 0.0, halo)
    hcat = jnp.concatenate([halo, h], axis=0).astype(BF16)
    proj = _dot(hcat, wcat_ref[...])

    qkv_pre = proj[:, :qkv_w]
    conv = qkv_pre * convw_ref[CONV_K - 1:CONV_K, :]
    for s in range(1, CONV_K):
        conv = conv + pltpu.roll(qkv_pre, s, 0) * convw_ref[CONV_K - 1 - s:CONV_K - s, :]
    qkv = _silu(conv[HALO:, :])

    parts = []
    for part, scale in ((0, DN_HEAD_DIM ** -0.5), (1, None), (2, None)):
        heads = []
        for hh in range(DN_HEADS):
            lo = part * DN_WIDTH + hh * DN_HEAD_DIM
            seg = qkv[:, lo:lo + DN_HEAD_DIM]
            if part < 2:
                inv = lax.rsqrt(jnp.sum(seg * seg, axis=-1, keepdims=True) + L2_EPS)
                seg = seg * (inv if scale is None else inv * scale)
            heads.append(seg)
        parts.append(jnp.concatenate(heads, axis=1))

    ab = proj[HALO:, qkv_w + POOL_WIDTH:]
    xa = ab + dtb_ref[...]
    softplus = jnp.maximum(xa, 0.0) + jnp.log1p(jnp.exp(-jnp.abs(xa)))
    gval = -jnp.exp(alog_ref[...]) * softplus
    lane = lax.broadcasted_iota(jnp.int32, ab.shape, 1)
    gb = jnp.where(lane < DN_HEADS, gval, jax.nn.sigmoid(ab))

    t = (t0 + lax.broadcasted_iota(jnp.int32, (n, 1), 0) + 1).astype(F32)
    pooled = []
    for gi, win in enumerate(POOL_WINDOWS):
        lo = qkv_w + gi * POOL_GROUP_DIM
        seg = proj[:, lo:lo + POOL_GROUP_DIM]
        wsum = seg
        sh = 1
        while sh < win:
            wsum = wsum + pltpu.roll(wsum, sh, 0)
            sh *= 2
        inv_cnt = 1.0 / jnp.minimum(t, float(win))
        mixed = wsum[HALO:, :] * inv_cnt - seg[HALO:, :]
        y = _dot(mixed.astype(BF16), poolw_ref[gi])
        cs = slice(gi * POOL_GROUP_DIM, (gi + 1) * POOL_GROUP_DIM)
        pooled.append((y * pscale_ref[:, cs]).astype(BF16))
    return parts[0], parts[1], parts[2], gb, jnp.concatenate(pooled, axis=1)


def _delta_stages(q_ref, k_ref, v_ref, gb_ref, o_ref, state_ref):
    c_len = DELTA_CHUNK
    bsz = q_ref.shape[0]
    n_chunks = q_ref.shape[1] // c_len
    ri = lax.broadcasted_iota(jnp.int32, (c_len, c_len), 0)
    ci = lax.broadcasted_iota(jnp.int32, (c_len, c_len), 1)
    incl = ri >= ci
    eye = ri == ci
    level_masks = []
    s = 1
    while s < c_len:
        level_masks.append(((ri // (2 * s)) == (ci // (2 * s))) & ((ri // s) % 2 == 1) & ((ci // s) % 2 == 0))
        s *= 2
    lane = lax.broadcasted_iota(jnp.int32, (c_len, LANES), 1)

    chains = []
    for c in range(n_chunks):
        rows = slice(c * c_len, (c + 1) * c_len)
        for b in range(bsz):
            gbc = gb_ref[b, rows, :]
            g_only = jnp.where(lane < DN_HEADS, gbc, 0.0)
            gc = jnp.dot(incl.astype(F32), g_only, precision=lax.Precision.HIGHEST,
                         preferred_element_type=F32)
            gct = gc.T
            e_gc = jnp.exp(gc)
            for hh in range(DN_HEADS):
                cs = slice(hh * DN_HEAD_DIM, (hh + 1) * DN_HEAD_DIM)
                chains.append(dict(
                    c=c, b=b, idx=b * DN_HEADS + hh, rows=rows, cs=cs,
                    q=q_ref[b, rows, cs], k=k_ref[b, rows, cs], v=v_ref[b, rows, cs],
                    beta=gbc[:, DN_HEADS + hh:DN_HEADS + hh + 1],
                    gcol=gc[:, hh:hh + 1],
                    grow=gct[hh:hh + 1, :],
                    glast=gct[hh:hh + 1, c_len - 1:c_len],
                    ecol=e_gc[:, hh:hh + 1]))

    for ch in chains:
        kh = ch["k"]
        ch["decay"] = jnp.exp(jnp.where(incl, ch["gcol"] - ch["grow"], -jnp.inf))
        ch["kb"] = kh * ch["beta"]
        ch["khb"] = kh.astype(BF16)
    for ch in chains:
        both = _dot_nt(jnp.concatenate([ch["kb"].astype(BF16), ch["q"].astype(BF16)], axis=0), ch["khb"])
        ch["a"] = jnp.where(eye, 0.0, both[:c_len] * ch["decay"])
        ch["attn"] = (both[c_len:] * ch["decay"]).astype(BF16)

    for ch in chains:
        ch["dinv"] = jnp.where(eye, 1.0, 0.0) - jnp.where(level_masks[0], ch["a"], 0.0)
    for m in level_masks[1:]:
        for ch in chains:
            ch["db"] = ch["dinv"].astype(BF16)
            ch["left"] = _dot(ch["db"], jnp.where(m, ch["a"], 0.0).astype(BF16))
        for ch in chains:
            ch["dinv"] = ch["dinv"] - _dot(ch["left"].astype(BF16), ch["db"])

    for ch in chains:
        t_low = jnp.where(eye, 0.0, ch["dinv"]).astype(BF16)
        rhs = jnp.concatenate([ch["v"] * ch["beta"], ch["kb"] * ch["ecol"]], axis=1)
        sol = rhs + _dot(t_low, rhs.astype(BF16))
        ch["u"] = sol[:, :DN_HEAD_DIM]
        qh = ch["q"]
        ch["wq"] = jnp.concatenate([sol[:, DN_HEAD_DIM:].astype(BF16), (qh * ch["ecol"]).astype(BF16)], axis=0)
        kdt = (ch["k"].T * jnp.exp(ch["glast"] - ch["grow"])).astype(BF16)
        ch["attn_kdt"] = jnp.concatenate([ch["attn"], kdt], axis=0)

    for c in range(n_chunks):
        group = [ch for ch in chains if ch["c"] == c]
        for ch in group:
            ch["proj"] = _dot(ch["wq"], state_ref[ch["idx"]].astype(BF16))
        for ch in group:
            ch["v_new"] = (ch["u"] - ch["proj"][:c_len]).astype(BF16)
        for ch in group:
            idx = ch["idx"]
            upd = _dot(ch["attn_kdt"], ch["v_new"])
            o_ref[ch["b"], ch["rows"], ch["cs"]] = ch["proj"][c_len:] + upd[:c_len]
            state_ref[idx] = state_ref[idx] * jnp.exp(ch["glast"]) + upd[c_len:]


def _mixer_proj_kernel(h_ref, halo_ref, wcat_ref, convw_ref, alog_ref, dtb_ref, poolw_ref, pscale_ref,
                       q_ref, k_ref, v_ref, gb_ref, yp_ref):
    i = pl.program_id(1)
    outs = _project_rows(h_ref[0], halo_ref[0], i == 0, i * h_ref.shape[1],
                         wcat_ref, convw_ref, alog_ref, dtb_ref, poolw_ref, pscale_ref)
    for ref, val in zip((q_ref, k_ref, v_ref, gb_ref, yp_ref), outs):
        ref[0] = val


def _mixer_proj(h, wcat, conv_w, alog_pad, dtb_pad, pool_w, pool_scale):
    bsz, seq, d = h.shape
    tm = min(TOKEN_TILE, seq)
    hb = tm // HALO
    row = lambda w: pl.BlockSpec((1, tm, w), lambda b, i: (b, i, 0))
    out_f = jax.ShapeDtypeStruct((bsz, seq, DN_WIDTH), F32)
    consts = (wcat, conv_w, alog_pad, dtb_pad, pool_w, pool_scale)
    return pl.pallas_call(
        _mixer_proj_kernel,
        grid=(bsz, seq // tm),
        in_specs=[row(d), pl.BlockSpec((1, HALO, d), lambda b, i: (b, jnp.maximum(i * hb - 1, 0), 0))]
                 + [_const_spec(c.shape) for c in consts],
        out_specs=[row(DN_WIDTH), row(DN_WIDTH), row(DN_WIDTH), row(AB_PAD), row(POOL_WIDTH)],
        out_shape=[out_f, out_f, out_f,
                   jax.ShapeDtypeStruct((bsz, seq, AB_PAD), F32),
                   jax.ShapeDtypeStruct((bsz, seq, POOL_WIDTH), BF16)],
        compiler_params=_params("parallel", "parallel"),
        name="mixer_proj",
    )(h, h, *consts)


def _delta_kernel(n_riders, q_ref, k_ref, v_ref, gb_ref, *refs):
    rider_in, o_ref, rider_out, state_ref = refs[:n_riders], refs[n_riders], refs[n_riders + 1:-1], refs[-1]

    @pl.when(pl.program_id(0) == 0)
    def _():
        state_ref[...] = jnp.zeros_like(state_ref)

    for src, dst in zip(rider_in, rider_out):
        dst[...] = src[...].astype(BF16)
    _delta_stages(q_ref, k_ref, v_ref, gb_ref, o_ref, state_ref)


def _delta_rule(q, k, v, gb, riders=()):
    bsz, seq, width = q.shape
    rows = min(DELTA_STEP_CHUNKS * DELTA_CHUNK, seq)
    blk = lambda w: pl.BlockSpec((bsz, rows, w), lambda n: (0, n, 0))
    rider_specs = _rider_specs(riders, seq // rows)
    outs = pl.pallas_call(
        functools.partial(_delta_kernel, len(riders)),
        grid=(seq // rows,),
        in_specs=[blk(width), blk(width), blk(width), blk(AB_PAD)] + rider_specs,
        out_specs=[blk(width)] + rider_specs,
        out_shape=[jax.ShapeDtypeStruct((bsz, seq, width), F32)]
                  + [jax.ShapeDtypeStruct(w.shape, BF16) for w in riders],
        scratch_shapes=[pltpu.VMEM((bsz * DN_HEADS, DN_HEAD_DIM, DN_HEAD_DIM), F32)],
        compiler_params=_params("arbitrary"),
        name="delta_rule",
    )(q, k, v, gb, *riders)
    return outs[0], outs[1:]


def _mixer_out_kernel(h_ref, o_ref, yp_ref, wz_ref, wgd_ref, wgp_ref, wdn_ref, wpool_ref, wmix_ref,
                      dnw_ref, g_ref, b_ref, out_ref):
    def project(rows):
        hb = h_ref[rows, :].astype(BF16)
        return (_dot(hb, wz_ref[...]), _dot(hb, wgd_ref[...]), _dot(hb, wgp_ref[...]),
                _dot(yp_ref[rows, :], wpool_ref[...]))

    def finish(rows, z, gate_dn, gate_pool, y_pool):
        o = o_ref[rows, :]
        gated = []
        for hh in range(DN_HEADS):
            cs = slice(hh * DN_HEAD_DIM, (hh + 1) * DN_HEAD_DIM)
            oh = o[:, cs]
            oh = oh * lax.rsqrt(jnp.mean(oh * oh, axis=-1, keepdims=True) + RMS_EPS)
            gated.append((oh * dnw_ref[...] * _silu(z[:, cs])).astype(BF16))
        y_dn = _dot(jnp.concatenate(gated, axis=1), wdn_ref[...])
        merged = jax.nn.sigmoid(gate_dn) * y_dn + jax.nn.sigmoid(gate_pool) * y_pool
        mix = _dot(merged.astype(BF16), wmix_ref[...])
        out_ref[rows, :] = _layernorm(ALPHA * h_ref[rows, :] + mix, g_ref[...], b_ref[...])

    groups = _row_groups(h_ref.shape[0])
    ahead = project(groups[0])
    for n, rows in enumerate(groups):
        current = ahead
        if n + 1 < len(groups):
            ahead = project(groups[n + 1])
        finish(rows, *current)


def _mixer_out(h2d, o2d, yp2d, wz, wgd, wgp, wdn, wpool, wmix, dnw, g, b):
    t, d = h2d.shape
    tm = min(TOKEN_TILE, t)
    row = lambda w: pl.BlockSpec((tm, w), lambda i: (i, 0))
    consts = (wz, wgd, wgp, wdn, wpool, wmix, dnw, g, b)
    return pl.pallas_call(
        _mixer_out_kernel,
        grid=(t // tm,),
        in_specs=[row(d), row(DN_WIDTH), row(POOL_WIDTH)] + [_const_spec(c.shape) for c in consts],
        out_specs=row(d),
        out_shape=jax.ShapeDtypeStruct((t, d), F32),
        compiler_params=_params("parallel"),
        name="mixer_out",
    )(h2d, o2d, yp2d, *consts)


def _xattn_ffn_kernel(x_ref, mem_ref, gm_ref, bm_ref, wk_ref, wv_ref, wq_ref, wo_ref, g3_ref, b3_ref,
                      wg_ref, wu_ref, wd_ref, g4_ref, b4_ref, o_ref, k_ref, v_ref):
    @pl.when(pl.program_id(1) == 0)
    def _():
        m = _layernorm(mem_ref[0], gm_ref[...], bm_ref[...]).astype(BF16)
        k_ref[0] = _dot(m, wk_ref[...]).astype(BF16)
        v_ref[0] = _dot(m, wv_ref[...]).astype(BF16)

    tm, d = x_ref.shape[1], x_ref.shape[2]
    hd = d // XA_HEADS
    cols = [slice(hh * hd, (hh + 1) * hd) for hh in range(XA_HEADS)]
    groups = _row_groups(tm)

    def attend(x, group_scores):
        heads = []
        for cs, s in zip(cols, group_scores):
            e = jnp.exp(s - jnp.max(s, axis=-1, keepdims=True))
            pr = e * (1.0 / jnp.sum(e, axis=-1, keepdims=True))
            heads.append(_dot(pr.astype(BF16), v_ref[0, :, cs]).astype(BF16))
        xa = _dot(jnp.concatenate(heads, axis=1), wo_ref[...])
        return _layernorm(ALPHA * x + xa, g3_ref[...], b3_ref[...])

    xs, scores = [], []
    for rows in groups:
        x = x_ref[0, rows, :]
        q = _dot(x.astype(BF16), wq_ref[...])
        xs.append(x)
        scores.append([_dot_nt(q[:, cs].astype(BF16), k_ref[0, :, cs]) * (hd ** -0.5) for cs in cols])
    ahead = attend(xs[0], scores[0])
    for n, rows in enumerate(groups):
        x3 = ahead
        if n + 1 < len(groups):
            ahead = attend(xs[n + 1], scores[n + 1])
        o_ref[0, rows, :] = _ffn_ln_rows(x3, wg_ref, wu_ref, wd_ref, g4_ref, b4_ref)


def _xattn_ffn(x, mem, gm, bm, wk, wv, wq, wo, g3, b3, wg, wu, wd, g4, b4):
    bsz, seq, d = x.shape
    n_mem = mem.shape[1]
    tm = min(TOKEN_TILE, seq)
    row = pl.BlockSpec((1, tm, d), lambda bb, i: (bb, i, 0))
    consts = (gm, bm, wk, wv, wq, wo, g3, b3, wg, wu, wd, g4, b4)
    return pl.pallas_call(
        _xattn_ffn_kernel,
        grid=(bsz, seq // tm),
        in_specs=[row, pl.BlockSpec((1, n_mem, d), lambda bb, i: (bb, 0, 0))]
                 + [_const_spec(c.shape) for c in consts],
        out_specs=row,
        out_shape=jax.ShapeDtypeStruct((bsz, seq, d), F32),
        scratch_shapes=[pltpu.VMEM((1, n_mem, d), BF16), pltpu.VMEM((1, n_mem, d), BF16)],
        compiler_params=_params("arbitrary", "arbitrary"),
        name="xattn_ffn",
    )(x, mem, *consts)


def _row(vec):
    return vec.reshape(1, -1).astype(F32)


def _pad_lanes(vec):
    return jnp.zeros((1, AB_PAD), F32).at[0, :vec.shape[0]].set(vec.astype(F32))


def kernel(x, mem, ffn1_w_gate, ffn1_w_up, ffn1_w_down, ln1_g, ln1_b, w_in, conv_w, a_log, dt_bias, dn_norm_w, w_dn_branch, pool_w, pool_scale, w_pool_branch, w_mix_out, ln2_g, ln2_b, mem_ln_g, mem_ln_b, xa_wq, xa_wk, xa_wv, xa_wo, ln3_g, ln3_b, ffn2_w_gate, ffn2_w_up, ffn2_w_down, ln4_g, ln4_b):
    bsz, seq, d = x.shape
    t = bsz * seq
    bf = lambda w: w.astype(BF16)

    for l in range(ffn1_w_gate.shape[0]):
        x2d, (wcat, w_z, w_gdn, w_gpool) = _ffn_ln(
            x.reshape(t, d), ffn1_w_gate[l].astype(F32), ffn1_w_up[l].astype(F32), ffn1_w_down[l].astype(F32),
            _row(ln1_g[l]), _row(ln1_b[l]), w_in_t=jnp.swapaxes(w_in[l], 0, 1))

        q, k, v, gb, ypool = _mixer_proj(x2d.reshape(bsz, seq, d), wcat, conv_w[l].astype(F32),
                                         _pad_lanes(a_log[l]), _pad_lanes(dt_bias[l]),
                                         bf(pool_w[l]), _row(pool_scale[l]))
        later = [ffn2_w_gate[l], ffn2_w_up[l], ffn2_w_down[l], w_dn_branch[l], w_pool_branch[l], w_mix_out[l],
                 xa_wq[l], xa_wk[l], xa_wv[l], xa_wo[l]]
        o, (f2_gate, f2_up, f2_down, w_dn, w_pool, w_mix, wq, wk, wv, wo) = _delta_rule(q, k, v, gb, riders=later)
        x2d = _mixer_out(x2d, o.reshape(t, DN_WIDTH), ypool.reshape(t, POOL_WIDTH),
                         w_z, w_gdn, w_gpool, w_dn, w_pool, w_mix,
                         _row(dn_norm_w[l]), _row(ln2_g[l]), _row(ln2_b[l]))

        x = _xattn_ffn(x2d.reshape(bsz, seq, d), mem, _row(mem_ln_g[l]), _row(mem_ln_b[l]), wk, wv,
                       wq, wo, _row(ln3_g[l]), _row(ln3_b[l]),
                       f2_gate, f2_up, f2_down, _row(ln4_g[l]), _row(ln4_b[l]))
    return x
```
